```python
import jax, jax.numpy as jnp
from jax import lax
import numpy as np

D_MODEL = 1024
BATCH = 8
SEQ = 8192
DEPTH = 2

GRID_W = 64
CTX_LEN = 256
D_MIX = D_MODEL
D_CONV = D_MIX // 2
CONV_W = 3
D_MLSTM = D_MIX - D_CONV
N_HEADS = 4
HEAD_DIM = D_MLSTM // N_HEADS
CHUNK = 128
N_EXPERTS = 16
CAPACITY_FACTOR = 2
D_EXPERT = 1408
N_MOD = 6
EPS = 1e-6
D_PROJ = 3 * D_CONV + 4 * D_MLSTM + 4 * N_HEADS

kernel_name = "hybrid_conv_mlstm_ecmoe_dit"


def rmsnorm(x, g):
    xf = x.astype(jnp.float32)
    y = xf * lax.rsqrt(jnp.mean(xf * xf, axis=-1, keepdims=True) + EPS)
    return (y * g.astype(jnp.float32)).astype(x.dtype)


def split_mod(m):
    m = m.reshape(m.shape[0], N_MOD, D_MODEL)
    return [m[:, j, None, :] for j in range(N_MOD)]


def modulate(h, shift, scale):
    return h * (1 + scale) + shift


def split_proj(u):
    pts = [D_CONV, 2 * D_CONV, 3 * D_CONV,
           3 * D_CONV + D_MLSTM, 3 * D_CONV + 2 * D_MLSTM,
           3 * D_CONV + 3 * D_MLSTM, 3 * D_CONV + 4 * D_MLSTM]
    return jnp.split(u, pts, axis=-1)


def dwconv3(u, w, b):
    ch = u.shape[-1]
    y = lax.conv_general_dilated(u, w[:, None, :].astype(u.dtype), window_strides=(1,),
                                 padding=((CONV_W // 2, CONV_W // 2),),
                                 dimension_numbers=('NWC', 'WIO', 'NWC'), feature_group_count=ch)
    return y + b.astype(u.dtype)


def conv_rows(u, w, b):
    bn, n, ch = u.shape
    rows = n // GRID_W
    return dwconv3(u.reshape(bn * rows, GRID_W, ch), w, b).reshape(bn, n, ch)


def mlstm_inputs(q, k, v, gates, gate_b):
    bn, n, _ = q.shape
    def heads(t):
        return t.reshape(bn, n, N_HEADS, HEAD_DIM).transpose(0, 2, 1, 3).astype(jnp.float32)
    g = (gates.astype(jnp.float32) + gate_b.reshape(-1).astype(jnp.float32))
    g = g.reshape(bn, n, 4, N_HEADS).transpose(2, 0, 3, 1)
    return heads(q) * HEAD_DIM ** -0.5, heads(k), heads(v), g


def mlstm_scan(q, k, v, i_pre, log_f, state):
    bn, nh, n, dh = q.shape
    nc = n // CHUNK
    def chunks(t):
        return jnp.moveaxis(t.reshape(bn, nh, nc, CHUNK, *t.shape[3:]), 2, 0)
    incl = jnp.tril(jnp.ones((CHUNK, CHUNK), dtype=bool))

    def body(carry, xs):
        C, nv, m = carry
        qb, kb, vb, ib, fb = xs
        b = jnp.cumsum(fb, axis=-1)
        a = b + m[..., None]
        dmat = jnp.where(incl, b[..., :, None] - b[..., None, :] + ib[..., None, :], -jnp.inf)
        m_t = jnp.maximum(a, jnp.max(dmat, axis=-1))
        w_intra = jnp.exp(dmat - m_t[..., None])
        w_inter = jnp.exp(a - m_t)
        s = jnp.einsum('bhtk,bhsk->bhts', qb, kb) * w_intra
        num = w_inter[..., None] * jnp.einsum('bhtk,bhkv->bhtv', qb, C) + jnp.einsum('bhts,bhsv->bhtv', s, vb)
        den = w_inter * jnp.einsum('bhtk,bhk->bht', qb, nv) + jnp.sum(s, axis=-1)
        h = num / jnp.maximum(jnp.abs(den), jnp.exp(-m_t))[..., None]
        m_new = m_t[..., -1]
        w_state = jnp.exp(b[..., -1] + m - m_new)
        w_tok = jnp.exp(b[..., -1:] - b + ib - m_new[..., None])
        C_new = w_state[..., None, None] * C + jnp.einsum('bhs,bhsk,bhsv->bhkv', w_tok, kb, vb)
        n_new = w_state[..., None] * nv + jnp.einsum('bhs,bhsk->bhk', w_tok, kb)
        return (C_new, n_new, m_new), h

    state, h = lax.scan(body, state, tuple(chunks(t) for t in (q, k, v, i_pre, log_f)))
    return jnp.moveaxis(h, 0, 2).reshape(bn, nh, n, dh), state


def mlstm_bidir(pc, pl, gate_b):
    qc, kc, vc, gc = mlstm_inputs(pc[0], pc[1], pc[2], pc[3], gate_b)
    ql, kl, vl, gl = mlstm_inputs(pl[0], pl[1], pl[2], pl[3], gate_b)
    bn = ql.shape[0]
    init = (jnp.zeros((bn, N_HEADS, HEAD_DIM, HEAD_DIM), jnp.float32),
            jnp.zeros((bn, N_HEADS, HEAD_DIM), jnp.float32),
            jnp.zeros((bn, N_HEADS), jnp.float32))
    outs_c, outs_l = [], []
    for d in range(2):
        flip = (lambda t: jnp.flip(t, axis=2)) if d == 1 else (lambda t: t)
        i_c, f_c = gc[2 * d], jax.nn.log_sigmoid(gc[2 * d + 1])
        i_l, f_l = gl[2 * d], jax.nn.log_sigmoid(gl[2 * d + 1])
        h_c, st = mlstm_scan(flip(qc), flip(kc), flip(vc), flip(i_c), flip(f_c), init)
        h_l, _ = mlstm_scan(flip(ql), flip(kl), flip(vl), flip(i_l), flip(f_l), st)
        outs_c.append(flip(h_c))
        outs_l.append(flip(h_l))
    return outs_c[0] + outs_c[1], outs_l[0] + outs_l[1]


def mlstm_out(h, o, g_head, dtype):
    hn = h * lax.rsqrt(jnp.mean(h * h, axis=-1, keepdims=True) + EPS) * g_head.astype(jnp.float32)[None, :, None, :]
    bn, nh, n, dh = h.shape
    hn = hn.transpose(0, 2, 1, 3).reshape(bn, n, nh * dh)
    return (jax.nn.sigmoid(o.astype(jnp.float32)) * hn).astype(dtype)


def mixer(p_self, h_mlstm, conv_fn, conv_w, conv_b, g_head, w_out):
    b_gate, c_gate, x_in, o = p_self[0], p_self[1], p_self[2], p_self[6]
    y_conv = b_gate * conv_fn(c_gate * x_in, conv_w, conv_b)
    y_mlstm = mlstm_out(h_mlstm, o, g_head, b_gate.dtype)
    return jnp.concatenate([y_conv, y_mlstm], axis=-1) @ w_out


def expert_choice_moe(h, w_router, w_gate, w_up, w_down):
    n = h.shape[1]
    cap = CAPACITY_FACTOR * n // N_EXPERTS

    def one_set(hs):
        aff = jax.nn.softmax(jnp.einsum('nd,de->en', hs, w_router).astype(jnp.float32), axis=0)
        g, idx = lax.top_k(aff, cap)
        xe = hs[idx]
        a = jnp.einsum('ecd,edf->ecf', xe, w_gate)
        u = jnp.einsum('ecd,edf->ecf', xe, w_up)
        ye = jnp.einsum('ecf,efd->ecd', jax.nn.silu(a) * u, w_down) * g[..., None].astype(hs.dtype)
        return jax.ops.segment_sum(ye.reshape(-1, hs.shape[-1]), idx.reshape(-1), num_segments=n)

    return lax.map(one_set, h)


def setup_inputs(seed: int = 0) -> dict:
    key = jax.random.key(seed)
    ks = jax.random.split(key, 24)
    def nrm(k, shape, s):
        return jax.random.normal(k, shape, jnp.float32) * s
    d = D_MODEL
    gate_b = jnp.stack([nrm(ks[10], (DEPTH, N_HEADS), 0.1),
                        3.0 + 3.0 * jax.random.uniform(ks[11], (DEPTH, N_HEADS), jnp.float32),
                        nrm(ks[12], (DEPTH, N_HEADS), 0.1),
                        3.0 + 3.0 * jax.random.uniform(ks[13], (DEPTH, N_HEADS), jnp.float32)], axis=1)
    return {
        "x": nrm(ks[0], (BATCH, SEQ, d), 1.0),
        "c": nrm(ks[1], (BATCH, d), 1.0),
        "ctx": nrm(ks[2], (BATCH, CTX_LEN, d), 1.0),
        "c_ctx": nrm(ks[3], (d,), 1.0),
        "w_mod": nrm(ks[4], (DEPTH, d, N_MOD * d), 0.5 * d ** -0.5),
        "b_mod": nrm(ks[5], (DEPTH, N_MOD * d), 0.01),
        "g_norm1": 1.0 + nrm(ks[6], (DEPTH, d), 0.02),
        "g_norm2": 1.0 + nrm(ks[7], (DEPTH, d), 0.02),
        "w_in": nrm(ks[8], (DEPTH, d, D_PROJ), d ** -0.5),
        "w_out": nrm(ks[9], (DEPTH, D_MIX, d), D_MIX ** -0.5),
        "conv_w": nrm(ks[14], (DEPTH, CONV_W, D_CONV), CONV_W ** -0.5),
        "conv_b": nrm(ks[15], (DEPTH, D_CONV), 0.01),
        "gate_b": gate_b,
        "g_head": 1.0 + nrm(ks[16], (DEPTH, N_HEADS, HEAD_DIM), 0.02),
        "w_router": nrm(ks[17], (DEPTH, d, N_EXPERTS), d ** -0.5),
        "w_gate_e": nrm(ks[18], (DEPTH, N_EXPERTS, d, D_EXPERT), d ** -0.5),
        "w_up_e": nrm(ks[19], (DEPTH, N_EXPERTS, d, D_EXPERT), d ** -0.5),
        "w_down_e": nrm(ks[20], (DEPTH, N_EXPERTS, D_EXPERT, d), D_EXPERT ** -0.5),
        "g_final": 1.0 + nrm(ks[21], (d,), 0.02),
    }


def reference(x, c, ctx, c_ctx, w_mod, b_mod, g_norm1, g_norm2, w_in, w_out, conv_w, conv_b,
              gate_b, g_head, w_router, w_gate_e, w_up_e, w_down_e, g_final):
    xl, xc = x, ctx
    for layer in range(DEPTH):
        last = layer == DEPTH - 1
        ml = split_mod(jax.nn.silu(c) @ w_mod[layer] + b_mod[layer])
        mc = split_mod(jax.nn.silu(c_ctx)[None] @ w_mod[layer] + b_mod[layer])

        hl = modulate(rmsnorm(xl, g_norm1[layer]), ml[0], ml[1])
        hc = modulate(rmsnorm(xc, g_norm1[layer]), mc[0], mc[1])
        pl = split_proj(hl @ w_in[layer])
        pc = split_proj(hc @ w_in[layer])
        hm_c, hm_l = mlstm_bidir((pc[3], pc[4], pc[5], pc[7]), (pl[3], pl[4], pl[5], pl[7]), gate_b[layer])
        yl = mixer(pl, hm_l, conv_rows, conv_w[layer], conv_b[layer], g_head[layer], w_out[layer])
        xl = xl + ml[2] * yl
        if not last:
            yc = mixer(pc, hm_c, dwconv3, conv_w[layer], conv_b[layer], g_head[layer], w_out[layer])
            xc = xc + mc[2] * yc

        hl = modulate(rmsnorm(xl, g_norm2[layer]), ml[3], ml[4])
        xl = xl + ml[5] * expert_choice_moe(hl, w_router[layer], w_gate_e[layer], w_up_e[layer], w_down_e[layer])
        if not last:
            hc = modulate(rmsnorm(xc, g_norm2[layer]), mc[3], mc[4])
            xc = xc + mc[5] * expert_choice_moe(hc, w_router[layer], w_gate_e[layer], w_up_e[layer], w_down_e[layer])
    return rmsnorm(xl, g_final)
```

```python
import functools

import jax
import jax.numpy as jnp
from jax import lax
from jax.experimental import pallas as pl
from jax.experimental.pallas import tpu as pltpu

EPS = 1e-6
N_HEADS = 4
HEAD_DIM = 128
CHUNK = 128
N_EXPERTS = 16
CAPACITY_FACTOR = 2
N_MOD = 6
GRID_W = 64
LANES = 128
ROUTE_TILE = 256
SLOT_CHUNK = 64
FIRST_CHUNK = 128
SLOT_PAD = 128
VMEM_LIMIT = 56 * 1024 * 1024

F32 = jnp.float32
BF16 = jnp.bfloat16
HIGHEST = lax.Precision.HIGHEST


def _params(sem, vmem=VMEM_LIMIT):
    return pltpu.CompilerParams(dimension_semantics=sem, vmem_limit_bytes=vmem)


def _sigmoid(x):
    return 1.0 / (1.0 + jnp.exp(-x))


def _mod_kernel(c_ref, w_ref, b_ref, o_ref):
    c = c_ref[...]
    s = c * _sigmoid(c)
    o_ref[0] = jnp.dot(s, w_ref[0], precision=HIGHEST, preferred_element_type=F32) + b_ref[0]


def _modulation(cc, w_mod, b_mod):
    depth, d, nm = w_mod.shape
    rows = cc.shape[0]
    tn = 1536
    return pl.pallas_call(
        _mod_kernel,
        out_shape=jax.ShapeDtypeStruct((depth, rows, nm), F32),
        grid=(depth, nm // tn),
        in_specs=[pl.BlockSpec((rows, d), lambda l, j: (0, 0)),
                  pl.BlockSpec((1, d, tn), lambda l, j: (l, 0, j)),
                  pl.BlockSpec((1, 1, tn), lambda l, j: (l, 0, j))],
        out_specs=pl.BlockSpec((1, rows, tn), lambda l, j: (l, 0, j)),
        compiler_params=_params(("arbitrary", "arbitrary")),
        name="modulation",
    )(cc, w_mod, b_mod.reshape(depth, 1, nm))


def _inproj_kernel(x_ref, mod_ref, gn_ref, wa_ref, wb_ref, wg_ref, gb_ref,
                   convo_ref, qkv_ref, p_ref, pt_ref, *, tm):
    x = x_ref[0]
    m = mod_ref[0]
    ms = jnp.mean(x * x, axis=-1, keepdims=True)
    h = (x * lax.rsqrt(ms + EPS) * gn_ref[...]) * (1.0 + m[1:2]) + m[0:1]
    hb = h.astype(BF16)
    cw = 512
    for c in range(wa_ref.shape[1] // cw):
        convo_ref[0, :, c * cw:(c + 1) * cw] = jnp.dot(
            hb, wa_ref[:, c * cw:(c + 1) * cw], preferred_element_type=F32).astype(BF16)
    for c in range(wb_ref.shape[1] // cw):
        r = jnp.dot(hb, wb_ref[:, c * cw:(c + 1) * cw], preferred_element_type=F32)
        if c == 0:
            r = r * (HEAD_DIM ** -0.5)
        qkv_ref[0, :, c * cw:(c + 1) * cw] = r.astype(BF16)

    g = jnp.dot(hb, wg_ref[...], preferred_element_type=F32) + gb_ref[...]
    kind = lax.broadcasted_iota(jnp.int32, (1, LANES), 1) // N_HEADS
    logf = jnp.minimum(g, 0.0) - jnp.log1p(jnp.exp(-jnp.abs(g)))
    g = jnp.where((kind == 1) | (kind == 3), logf, g)
    row = lax.broadcasted_iota(jnp.int32, (CHUNK, CHUNK), 0)
    col = lax.broadcasted_iota(jnp.int32, (CHUNK, CHUNK), 1)
    tril = jnp.where(col <= row, 1.0, 0.0).astype(F32)
    triu = jnp.where(col >= row, 1.0, 0.0).astype(F32)
    for r in range(tm // CHUNK):
        gc = g[r * CHUNK:(r + 1) * CHUNK]
        cum = jnp.dot(tril, gc, precision=HIGHEST, preferred_element_type=F32)
        suf = jnp.dot(triu, gc, precision=HIGHEST, preferred_element_type=F32)
        p = jnp.where(kind == 1, cum, jnp.where(kind == 3, suf, gc))
        p_ref[0, r * CHUNK:(r + 1) * CHUNK, :] = p[:, :4 * N_HEADS]
        pt_ref[0, :, r * CHUNK:(r + 1) * CHUNK] = p.T[:4 * N_HEADS, :]


def _inproj(x, mod_l, mod_row, gn, wa, wb, wg, gb, tm):
    b, n, d = x.shape
    per_sample = mod_row is None
    mod_map = (lambda bi, i: (bi, 0, 0)) if per_sample else (lambda bi, i: (mod_row, 0, 0))
    ng = 4 * N_HEADS
    return pl.pallas_call(
        functools.partial(_inproj_kernel, tm=tm),
        out_shape=(jax.ShapeDtypeStruct((b, n, wa.shape[1]), BF16),
                   jax.ShapeDtypeStruct((b, n, wb.shape[1]), BF16),
                   jax.ShapeDtypeStruct((b, n, ng), F32),
                   jax.ShapeDtypeStruct((b, ng, n), F32)),
        grid=(b, n // tm),
        in_specs=[pl.BlockSpec((1, tm, d), lambda bi, i: (bi, i, 0)),
                  pl.BlockSpec((1, N_MOD, d), mod_map),
                  pl.BlockSpec((1, d), lambda bi, i: (0, 0)),
                  pl.BlockSpec(wa.shape, lambda bi, i: (0, 0)),
                  pl.BlockSpec(wb.shape, lambda bi, i: (0, 0)),
                  pl.BlockSpec(wg.shape, lambda bi, i: (0, 0)),
                  pl.BlockSpec((1, LANES), lambda bi, i: (0, 0))],
        out_specs=(pl.BlockSpec((1, tm, wa.shape[1]), lambda bi, i: (bi, i, 0)),
                   pl.BlockSpec((1, tm, wb.shape[1]), lambda bi, i: (bi, i, 0)),
                   pl.BlockSpec((1, tm, ng), lambda bi, i: (bi, i, 0)),
                   pl.BlockSpec((1, ng, tm), lambda bi, i: (bi, 0, i))),
        compiler_params=_params(("arbitrary", "arbitrary")),
        name="inproj",
    )(x, mod_l, gn, wa, wb, wg, gb)


def _mlstm_kernel(qf_ref, qb_ref, pf_ref, pb_ref, ptf_ref, ptb_ref, c0_ref, n0_ref, m0_ref,
                  hf_ref, hb_ref, c1_ref, n1_ref, m1_ref, c_sc, n_sc, m_sc):
    j = pl.program_id(1)
    nc = pl.num_programs(1)
    L = CHUNK
    dm = N_HEADS * HEAD_DIM

    @pl.when(j == 0)
    def _():
        c_sc[...] = c0_ref[0]
        n_sc[...] = n0_ref[0]
        m_sc[...] = m0_ref[0]

    row = lax.broadcasted_iota(jnp.int32, (L, L), 0)
    col = lax.broadcasted_iota(jnp.int32, (L, L), 1)
    for d in range(2):
        qkv_ref = (qf_ref, qb_ref)[d]
        p = (pf_ref, pb_ref)[d][0]
        pt = (ptf_ref, ptb_ref)[d][0]
        out_ref = (hf_ref, hb_ref)[d]
        mask = (col <= row) if d == 0 else (col >= row)
        last = L - 1 if d == 0 else 0
        for hd in range(N_HEADS):
            idx = d * N_HEADS + hd
            ci = 2 * d * N_HEADS + hd
            cb = ci + N_HEADS
            q = qkv_ref[0, :, hd * HEAD_DIM:(hd + 1) * HEAD_DIM]
            k = qkv_ref[0, :, dm + hd * HEAD_DIM:dm + (hd + 1) * HEAD_DIM]
            v = qkv_ref[0, :, 2 * dm + hd * HEAD_DIM:2 * dm + (hd + 1) * HEAD_DIM]
            i_col, b_col = p[:, ci:ci + 1], p[:, cb:cb + 1]
            i_row, b_row = pt[ci:ci + 1, :], pt[cb:cb + 1, :]
            m_prev = m_sc[idx][:, 0:1]
            cmat = c_sc[idx]
            nvec = n_sc[idx]

            dmat = jnp.where(mask, b_col - b_row + i_row, -jnp.inf)
            a = b_col + m_prev
            m_t = jnp.maximum(a, jnp.max(dmat, axis=1, keepdims=True))
            w_intra = jnp.exp(dmat - m_t)
            w_inter = jnp.exp(a - m_t)
            s = lax.dot_general(q, k, (((1,), (1,)), ((), ())), preferred_element_type=F32) * w_intra
            num = (w_inter * jnp.dot(q, cmat.astype(BF16), preferred_element_type=F32)
                   + jnp.dot(s.astype(BF16), v, preferred_element_type=F32))
            den = (w_inter * jnp.sum(q.astype(F32) * nvec, axis=1, keepdims=True)
                   + jnp.sum(s, axis=1, keepdims=True))
            out_ref[0, :, hd * HEAD_DIM:(hd + 1) * HEAD_DIM] = num / jnp.maximum(jnp.abs(den), jnp.exp(-m_t))

            m_new = m_t[last:last + 1, :]
            b_last = b_col[last:last + 1, :]
            w_state = jnp.exp(b_last + m_prev - m_new)
            w_tok = jnp.exp(b_last - b_col + i_col - m_new)
            kw = k.astype(F32) * w_tok
            c_sc[idx] = w_state * cmat + lax.dot_general(
                kw.astype(BF16), v, (((0,), (0,)), ((), ())), preferred_element_type=F32)
            n_sc[idx] = w_state * nvec + jnp.sum(kw, axis=0, keepdims=True)
            m_sc[idx] = jnp.broadcast_to(m_new, (1, LANES))

    @pl.when(j == nc - 1)
    def _():
        c1_ref[0] = c_sc[...]
        n1_ref[0] = n_sc[...]
        m1_ref[0] = m_sc[...]


def _mlstm(qkv, p, pt, state):
    b, n, w = qkv.shape
    nc = n // CHUNK
    ng = 4 * N_HEADS
    ns = 2 * N_HEADS
    dm = N_HEADS * HEAD_DIM
    c0, n0, m0 = state
    fwd = lambda bi, j: (bi, j, 0)
    bwd = lambda bi, j: (bi, nc - 1 - j, 0)
    st4 = lambda bi, j: (bi, 0, 0, 0)
    outs = pl.pallas_call(
        _mlstm_kernel,
        out_shape=(jax.ShapeDtypeStruct((b, n, dm), F32),
                   jax.ShapeDtypeStruct((b, n, dm), F32),
                   jax.ShapeDtypeStruct(c0.shape, F32),
                   jax.ShapeDtypeStruct(n0.shape, F32),
                   jax.ShapeDtypeStruct(m0.shape, F32)),
        grid=(b, nc),
        in_specs=[pl.BlockSpec((1, CHUNK, w), fwd),
                  pl.BlockSpec((1, CHUNK, w), bwd),
                  pl.BlockSpec((1, CHUNK, ng), fwd),
                  pl.BlockSpec((1, CHUNK, ng), bwd),
                  pl.BlockSpec((1, ng, CHUNK), lambda bi, j: (bi, 0, j)),
                  pl.BlockSpec((1, ng, CHUNK), lambda bi, j: (bi, 0, nc - 1 - j)),
                  pl.BlockSpec((1, ns, HEAD_DIM, HEAD_DIM), st4),
                  pl.BlockSpec((1, ns, 1, LANES), st4),
                  pl.BlockSpec((1, ns, 1, LANES), st4)],
        out_specs=(pl.BlockSpec((1, CHUNK, dm), fwd),
                   pl.BlockSpec((1, CHUNK, dm), bwd),
                   pl.BlockSpec((1, ns, HEAD_DIM, HEAD_DIM), st4),
                   pl.BlockSpec((1, ns, 1, LANES), st4),
                   pl.BlockSpec((1, ns, 1, LANES), st4)),
        scratch_shapes=[pltpu.VMEM((ns, HEAD_DIM, HEAD_DIM), F32),
                        pltpu.VMEM((ns, 1, LANES), F32),
                        pltpu.VMEM((ns, 1, LANES), F32)],
        compiler_params=_params(("arbitrary", "arbitrary")),
        name="mlstm",
    )(qkv, qkv, p, p, pt, pt, c0, n0, m0)
    return outs[0], outs[1], (outs[2], outs[3], outs[4])


def _mixout_kernel(convo_ref, hf_ref, hb_ref, x_ref, mod_ref, cw_ref, cb_ref, gh_ref, wo_ref, gn2_ref, wr_ref,
                   xo_ref, h2_ref, lt_ref, *, tm, row_len):
    dc = cw_ref.shape[1]
    cv = convo_ref[0]
    bg = cv[:, 0:dc].astype(F32)
    cg = cv[:, dc:2 * dc].astype(F32)
    xi = cv[:, 2 * dc:3 * dc].astype(F32)
    og = cv[:, 3 * dc:].astype(F32)
    u = cg * xi
    t = lax.broadcasted_iota(jnp.int32, (tm, dc), 0) % row_len
    u_prev = jnp.where(t == 0, 0.0, pltpu.roll(u, 1, axis=0))
    u_next = jnp.where(t == row_len - 1, 0.0, pltpu.roll(u, tm - 1, axis=0))
    cw = cw_ref[...]
    yc = bg * (cw[0:1] * u_prev + cw[1:2] * u + cw[2:3] * u_next + cb_ref[...])

    hm = hf_ref[0] + hb_ref[0]
    parts = []
    for hd in range(N_HEADS):
        hh = hm[:, hd * HEAD_DIM:(hd + 1) * HEAD_DIM]
        parts.append(hh * lax.rsqrt(jnp.mean(hh * hh, axis=-1, keepdims=True) + EPS))
    ym = _sigmoid(og) * (jnp.concatenate(parts, axis=-1) * gh_ref[...])

    cat = jnp.concatenate([yc, ym], axis=-1).astype(BF16)
    y = jnp.dot(cat, wo_ref[...], preferred_element_type=F32)
    m = mod_ref[0]
    xn = x_ref[0] + m[2:3] * y
    xo_ref[0] = xn
    ms = jnp.mean(xn * xn, axis=-1, keepdims=True)
    h2 = (xn * lax.rsqrt(ms + EPS) * gn2_ref[...]) * (1.0 + m[4:5]) + m[3:4]
    h2_ref[0] = h2.astype(BF16)
    lg = jnp.dot(h2, wr_ref[...], precision=HIGHEST, preferred_element_type=F32)
    for r in range(tm // LANES):
        lt_ref[0, :, r * LANES:(r + 1) * LANES] = lg[r * LANES:(r + 1) * LANES, :].T[:N_EXPERTS, :]


def _mixout(convo, hf, hb, x, mod_l, mod_row, cw, cb, gh, wo, gn2, wr, tm, row_len):
    b, n, d = x.shape
    per_sample = mod_row is None
    mod_map = (lambda bi, i: (bi, 0, 0)) if per_sample else (lambda bi, i: (mod_row, 0, 0))
    tok = lambda bi, i: (bi, i, 0)
    cst = lambda bi, i: (0, 0)
    return pl.pallas_call(
        functools.partial(_mixout_kernel, tm=tm, row_len=row_len),
        out_shape=(jax.ShapeDtypeStruct((b, n, d), F32),
                   jax.ShapeDtypeStruct((b, n, d), BF16),
                   jax.ShapeDtypeStruct((b, N_EXPERTS, n), F32)),
        grid=(b, n // tm),
        in_specs=[pl.BlockSpec((1, tm, convo.shape[2]), tok),
                  pl.BlockSpec((1, tm, hf.shape[2]), tok),
                  pl.BlockSpec((1, tm, hb.shape[2]), tok),
                  pl.BlockSpec((1, tm, d), tok),
                  pl.BlockSpec((1, N_MOD, d), mod_map),
                  pl.BlockSpec(cw.shape, cst),
                  pl.BlockSpec(cb.shape, cst),
                  pl.BlockSpec(gh.shape, cst),
                  pl.BlockSpec(wo.shape, cst),
                  pl.BlockSpec(gn2.shape, cst),
                  pl.BlockSpec(wr.shape, cst)],
        out_specs=(pl.BlockSpec((1, tm, d), tok),
                   pl.BlockSpec((1, tm, d), tok),
                   pl.BlockSpec((1, N_EXPERTS, tm), lambda bi, i: (bi, 0, i))),
        compiler_params=_params(("arbitrary", "arbitrary")),
        name="mixout",
    )(convo, hf, hb, x, mod_l, cw, cb, gh, wo, gn2, wr)


def _route_kernel(lt_ref, pos_ref, g_ref, cs_ref, aff_sc, *, n, cap):
    lg = lt_ref[0]
    e = jnp.exp(lg - jnp.max(lg, axis=0, keepdims=True))
    aff_sc[...] = e / jnp.sum(e, axis=0, keepdims=True)
    capf = jnp.float32(cap)

    def count_ge(cand):
        acc = jnp.zeros((N_EXPERTS, LANES), F32)
        for j in range(n // LANES):
            acc = acc + jnp.where(aff_sc[:, j * LANES:(j + 1) * LANES] >= cand, 1.0, 0.0)
        return jnp.sum(acc, axis=1, keepdims=True)

    def exp_step(_, c):
        elo, ehi = c
        emid = jnp.floor((elo + ehi) * 0.5)
        ok = count_ge(jnp.exp2(emid)) >= capf
        return jnp.where(ok, emid, elo), jnp.where(ok, ehi, emid)

    elo, ehi = lax.fori_loop(0, 8, exp_step, (jnp.full((N_EXPERTS, 1), -128.0, F32), jnp.ones((N_EXPERTS, 1), F32)))

    def val_step(_, c):
        lo, hi = c
        mid = lo + (hi - lo) * 0.5
        ok = count_ge(mid) >= capf
        return jnp.where(ok, mid, lo), jnp.where(ok, hi, mid)

    lo, hi = lax.fori_loop(0, 40, val_step, (jnp.where(elo <= -128.0, 0.0, jnp.exp2(elo)), jnp.exp2(ehi)))
    need = capf - count_ge(hi)

    row = lax.broadcasted_iota(jnp.int32, (LANES, LANES), 0)
    col = lax.broadcasted_iota(jnp.int32, (LANES, LANES), 1)
    upper = jnp.where(row < col, 1.0, 0.0).astype(BF16)
    lane = lax.broadcasted_iota(jnp.int32, (N_EXPERTS, LANES), 1)
    per_tile = ROUTE_TILE // LANES

    cs_ref[0] = jnp.zeros((N_EXPERTS, LANES), jnp.int32)

    def block(c, carry):
        ceq, csel = carry
        off = pl.multiple_of(c * LANES, LANES)
        a = aff_sc[:, pl.ds(off, LANES)]
        tie = (a >= lo) & (a < hi)
        eq = jnp.where(tie, 1.0, 0.0)
        peq = jnp.dot(eq.astype(BF16), upper, preferred_element_type=F32) + ceq
        sel = (a >= hi) | (tie & (peq < need))
        self = jnp.where(sel, 1.0, 0.0)
        psel = jnp.dot(self.astype(BF16), upper, preferred_element_type=F32) + csel
        pos_ref[0, :, pl.ds(off, LANES)] = jnp.where(sel, psel.astype(jnp.int32), -1)
        g_ref[0, :, pl.ds(off, LANES)] = jnp.where(sel, a, 0.0)
        tile_lane = jnp.where(c % per_tile == 0, c // per_tile, -1)
        cs_ref[0] = jnp.where(lane == tile_lane, csel.astype(jnp.int32), cs_ref[0])
        return ceq + jnp.sum(eq, axis=1, keepdims=True), csel + jnp.sum(self, axis=1, keepdims=True)

    zero = jnp.zeros((N_EXPERTS, 1), F32)
    _, csel = lax.fori_loop(0, n // LANES, block, (zero, zero))
    cs_ref[0] = jnp.where(lane >= n // ROUTE_TILE, csel.astype(jnp.int32), cs_ref[0])


def _route(lt):
    b, ne, n = lt.shape
    cap = CAPACITY_FACTOR * n // N_EXPERTS
    blk = pl.BlockSpec((1, ne, n), lambda bi: (bi, 0, 0))
    return pl.pallas_call(
        functools.partial(_route_kernel, n=n, cap=cap),
        out_shape=(jax.ShapeDtypeStruct((b, ne, n), jnp.int32),
                   jax.ShapeDtypeStruct((b, ne, n), F32),
                   jax.ShapeDtypeStruct((b, ne, LANES), jnp.int32)),
        grid=(b,),
        in_specs=[blk],
        out_specs=(blk, blk, pl.BlockSpec((1, ne, LANES), lambda bi: (bi, 0, 0))),
        scratch_shapes=[pltpu.VMEM((ne, n), F32)],
        compiler_params=_params(("arbitrary",)),
        name="route",
    )(lt)


def _ffn_kernel(cs_ref, h2_ref, pos_ref, wg_ref, wu_ref, wd_ref, ye_ref, xe_sc, *, tdma, nt_pad, cap, row_chunk):
    e = pl.program_id(0)
    b = pl.program_id(1)
    s = pl.program_id(2)
    ns = pl.num_programs(2)
    sub = tdma // ROUTE_TILE

    @pl.when(s == 0)
    def _():
        xe_sc[...] = jnp.zeros_like(xe_sc)

    cs_base = (b * N_EXPERTS + e) * nt_pad + s * sub

    jrow1 = lax.broadcasted_iota(jnp.int32, (FIRST_CHUNK, ROUTE_TILE), 0)
    for i in range(sub):
        base = pl.multiple_of((cs_ref[cs_base + i] // 8) * 8, 8)
        prow = pos_ref[0, 0, :, i * ROUTE_TILE:(i + 1) * ROUTE_TILE]
        xt = h2_ref[0, i * ROUTE_TILE:(i + 1) * ROUTE_TILE, :]
        onehot = jnp.where(prow == base + jrow1, 1.0, 0.0).astype(BF16)
        xe_sc[pl.ds(base, FIRST_CHUNK), :] += jnp.dot(onehot, xt, preferred_element_type=F32)

    jrow = lax.broadcasted_iota(jnp.int32, (SLOT_CHUNK, ROUTE_TILE), 0)

    def tile_body(i, carry):
        first_end = (cs_ref[cs_base + i] // 8) * 8 + FIRST_CHUNK
        c1 = cs_ref[cs_base + i + 1]
        nch = jnp.where(c1 > first_end, (c1 - first_end + SLOT_CHUNK - 1) // SLOT_CHUNK, 0)
        off = pl.multiple_of(i * ROUTE_TILE, ROUTE_TILE)

        def chunk(r, c):
            rb = pl.multiple_of(first_end + r * SLOT_CHUNK, 8)
            prow = pos_ref[0, 0, :, pl.ds(off, ROUTE_TILE)]
            onehot = jnp.where(prow == rb + jrow, 1.0, 0.0).astype(BF16)
            xe_sc[pl.ds(rb, SLOT_CHUNK), :] += jnp.dot(onehot, h2_ref[0, pl.ds(off, ROUTE_TILE), :],
                                                       preferred_element_type=F32)
            return c

        return lax.fori_loop(0, nch, chunk, carry)

    lax.fori_loop(0, sub, tile_body, 0)

    @pl.when(s == ns - 1)
    def _():
        for rc in range(cap // row_chunk):
            rows = slice(rc * row_chunk, (rc + 1) * row_chunk)
            xb = xe_sc[rows, :].astype(BF16)
            a = jnp.dot(xb, wg_ref[0], preferred_element_type=F32)
            u = jnp.dot(xb, wu_ref[0], preferred_element_type=F32)
            hm = (a * _sigmoid(a) * u).astype(BF16)
            ye_ref[0, 0, rows, :] = jnp.dot(hm, wd_ref[0], preferred_element_type=F32).astype(BF16)
        ye_ref[0, 0, cap:, :] = jnp.zeros((ye_ref.shape[2] - cap, ye_ref.shape[3]), BF16)


def _ffn(cs_flat, nt_pad, h2, pos4, wg, wu, wd):
    b, n, d = h2.shape
    ne, _, f = wg.shape
    cap = CAPACITY_FACTOR * n // N_EXPERTS
    rows = cap + SLOT_PAD
    tdma = min(n, 2048)
    row_chunk = min(cap, 512)
    grid_spec = pltpu.PrefetchScalarGridSpec(
        num_scalar_prefetch=1,
        grid=(ne, b, n // tdma),
        in_specs=[pl.BlockSpec((1, tdma, d), lambda e, bi, s, cs: (bi, s, 0)),
                  pl.BlockSpec((1, 1, 1, tdma), lambda e, bi, s, cs: (bi, e, 0, s)),
                  pl.BlockSpec((1, d, f), lambda e, bi, s, cs: (e, 0, 0)),
                  pl.BlockSpec((1, d, f), lambda e, bi, s, cs: (e, 0, 0)),
                  pl.BlockSpec((1, f, d), lambda e, bi, s, cs: (e, 0, 0))],
        out_specs=pl.BlockSpec((1, 1, rows, d), lambda e, bi, s, cs: (bi, e, 0, 0)),
        scratch_shapes=[pltpu.VMEM((rows, d), F32)])
    return pl.pallas_call(
        functools.partial(_ffn_kernel, tdma=tdma, nt_pad=nt_pad, cap=cap, row_chunk=row_chunk),
        out_shape=jax.ShapeDtypeStruct((b, ne, rows, d), BF16),
        grid_spec=grid_spec,
        compiler_params=_params(("arbitrary", "arbitrary", "arbitrary")),
        name="ffn",
    )(cs_flat, h2, pos4, wg, wu, wd)


def _combine_kernel(cs_ref, pos_ref, g_ref, x_ref, mod_ref, gfin_ref, ye_hbm, out_ref,
                    ystage, pt_sc, acc_sc, sem, *, nt_pad, final):
    b = pl.program_id(0)
    i = pl.program_id(1)
    nt = pl.num_programs(1)
    step = b * nt + i
    slot = step % 2
    group = 4 * SLOT_CHUNK

    @pl.when(step == 0)
    def _():
        ystage[...] = jnp.zeros_like(ystage)

    def chunk_plan(bb, ii, e):
        c0 = cs_ref[(bb * N_EXPERTS + e) * nt_pad + ii]
        c1 = cs_ref[(bb * N_EXPERTS + e) * nt_pad + ii + 1]
        base = (c0 // 16) * 16
        return base, jnp.where(c1 > c0, (c1 - base + SLOT_CHUNK - 1) // SLOT_CHUNK, 0)

    def chunk_copy(bb, e, rb, sl, k):
        return pltpu.make_async_copy(ye_hbm.at[bb, e, pl.ds(rb, SLOT_CHUNK), :],
                                     ystage.at[sl, pl.ds(k * SLOT_CHUNK, SLOT_CHUNK), :], sem.at[sl])

    def issue(bb, ii, sl):
        k = jnp.int32(0)
        for e in range(N_EXPERTS):
            base, nch = chunk_plan(bb, ii, e)

            def start(r, kk, e=e, base=base):
                chunk_copy(bb, e, pl.multiple_of(base + r * SLOT_CHUNK, 16), sl, kk).start()
                return kk + 1

            k = lax.fori_loop(0, nch, start, k)

    @pl.when(step == 0)
    def _():
        issue(b, i, slot)

    @pl.when(step + 1 < pl.num_programs(0) * nt)
    def _():
        wrap = i + 1 == nt
        issue(jnp.where(wrap, b + 1, b), jnp.where(wrap, 0, i + 1), 1 - slot)

    jrow = lax.broadcasted_iota(jnp.int32, (SLOT_CHUNK, ROUTE_TILE), 0)
    k = jnp.int32(0)
    for e in range(N_EXPERTS):
        base, nch = chunk_plan(b, i, e)
        prow = pos_ref[0, e:e + 1, :]
        grow = g_ref[0, e:e + 1, :]

        def weights(r, kk, base=base, prow=prow, grow=grow):
            row0 = pl.multiple_of(kk * SLOT_CHUNK, SLOT_CHUNK)
            pt_sc[pl.ds(row0, SLOT_CHUNK), :] = jnp.where(prow == base + r * SLOT_CHUNK + jrow, grow, 0.0)
            return kk + 1

        k = lax.fori_loop(0, nch, weights, k)

    kpad = ((k + 3) // 4) * 4

    def zero_pad(kk, c):
        row0 = pl.multiple_of(kk * SLOT_CHUNK, SLOT_CHUNK)
        pt_sc[pl.ds(row0, SLOT_CHUNK), :] = jnp.zeros((SLOT_CHUNK, ROUTE_TILE), F32)
        return c

    lax.fori_loop(k, kpad, zero_pad, 0)

    def wait_one(kk, c):
        chunk_copy(0, 0, 0, slot, kk).wait()
        return c

    lax.fori_loop(0, k, wait_one, 0)

    acc_sc[...] = jnp.zeros_like(acc_sc)

    def matmul_group(gi, c):
        row0 = pl.multiple_of(gi * group, group)
        w = pt_sc[pl.ds(row0, group), :].T.astype(BF16)
        acc_sc[...] += jnp.dot(w, ystage[slot, pl.ds(row0, group), :], preferred_element_type=F32)
        return c

    lax.fori_loop(0, kpad // 4, matmul_group, 0)

    xn = x_ref[0] + mod_ref[0][5:6] * acc_sc[...]
    if final:
        ms = jnp.mean(xn * xn, axis=-1, keepdims=True)
        xn = xn * lax.rsqrt(ms + EPS) * gfin_ref[...]
    out_ref[0] = xn


def _combine(cs_flat, nt_pad, pos, g, x, mod_l, mod_row, gfin, ye, final):
    b, n, d = x.shape
    per_sample = mod_row is None
    mod_map = ((lambda bi, i, cs: (bi, 0, 0)) if per_sample else (lambda bi, i, cs: (mod_row, 0, 0)))
    max_chunks = N_EXPERTS * ((ROUTE_TILE + 15 + SLOT_CHUNK - 1) // SLOT_CHUNK) + 4
    grid_spec = pltpu.PrefetchScalarGridSpec(
        num_scalar_prefetch=1,
        grid=(b, n // ROUTE_TILE),
        in_specs=[pl.BlockSpec((1, N_EXPERTS, ROUTE_TILE), lambda bi, i, cs: (bi, 0, i)),
                  pl.BlockSpec((1, N_EXPERTS, ROUTE_TILE), lambda bi, i, cs: (bi, 0, i)),
                  pl.BlockSpec((1, ROUTE_TILE, d), lambda bi, i, cs: (bi, i, 0)),
                  pl.BlockSpec((1, N_MOD, d), mod_map),
                  pl.BlockSpec((1, d), lambda bi, i, cs: (0, 0)),
                  pl.BlockSpec(memory_space=pl.ANY)],
        out_specs=pl.BlockSpec((1, ROUTE_TILE, d), lambda bi, i, cs: (bi, i, 0)),
        scratch_shapes=[pltpu.VMEM((2, max_chunks * SLOT_CHUNK, d), BF16),
                        pltpu.VMEM((max_chunks * SLOT_CHUNK, ROUTE_TILE), F32),
                        pltpu.VMEM((ROUTE_TILE, d), F32),
                        pltpu.SemaphoreType.DMA((2,))])
    return pl.pallas_call(
        functools.partial(_combine_kernel, nt_pad=nt_pad, final=final),
        out_shape=jax.ShapeDtypeStruct((b, n, d), F32),
        grid_spec=grid_spec,
        compiler_params=_params(("arbitrary", "arbitrary")),
        name="combine",
    )(cs_flat, pos, g, x, mod_l, gfin, ye)


def _moe(lt, h2, x, mod_l, mod_row, gfin, wg, wu, wd, final):
    b, n, _ = x.shape
    pos, g, cs = _route(lt)
    nt_pad = ((n // ROUTE_TILE + 1 + 7) // 8) * 8
    cs_flat = cs[:, :, :nt_pad].reshape(-1)
    ye = _ffn(cs_flat, nt_pad, h2, pos.reshape(b, N_EXPERTS, 1, n), wg, wu, wd)
    return _combine(cs_flat, nt_pad, pos, g, x, mod_l, mod_row, gfin, ye, final)


def kernel(x, c, ctx, c_ctx, w_mod, b_mod, g_norm1, g_norm2, w_in, w_out, conv_w, conv_b, gate_b, g_head,
           w_router, w_gate_e, w_up_e, w_down_e, g_final):
    bsz, n, d = x.shape
    nctx = ctx.shape[1]
    depth = w_mod.shape[0]
    dm = N_HEADS * HEAD_DIM
    dc = conv_w.shape[2]
    ng = 4 * N_HEADS
    ctx_row = bsz

    rows = ((bsz + 1 + 7) // 8) * 8
    cc = jnp.zeros((rows, d), F32).at[:bsz].set(c).at[bsz].set(c_ctx)
    mod = _modulation(cc, w_mod, b_mod).reshape(depth, rows, N_MOD, d)
    gfin = g_final.reshape(1, d)

    xl, xc = x, ctx
    for layer in range(depth):
        last = layer == depth - 1
        mod_l = mod[layer]
        wi = w_in[layer]
        wa = jnp.concatenate([wi[:, :3 * dc], wi[:, 3 * dc + 3 * dm:3 * dc + 4 * dm]], axis=1).astype(BF16)
        wb = wi[:, 3 * dc:3 * dc + 3 * dm].astype(BF16)
        wgt = jnp.zeros((d, LANES), BF16).at[:, :ng].set(wi[:, 3 * dc + 4 * dm:].astype(BF16))
        gb = jnp.zeros((1, LANES), F32).at[0, :ng].set(gate_b[layer].reshape(-1))
        gn1 = g_norm1[layer].reshape(1, d)
        gn2 = g_norm2[layer].reshape(1, d)
        cw = jnp.zeros((8, dc), F32).at[:conv_w.shape[1]].set(conv_w[layer])
        cb = conv_b[layer].reshape(1, dc)
        gh = g_head[layer].reshape(1, dm)
        wo = w_out[layer].astype(BF16)
        wr = jnp.zeros((d, LANES), F32).at[:, :N_EXPERTS].set(w_router[layer])
        wge = w_gate_e[layer].astype(BF16)
        wue = w_up_e[layer].astype(BF16)
        wde = w_down_e[layer].astype(BF16)

        convo_c, qkv_c, p_c, pt_c = _inproj(xc, mod_l, ctx_row, gn1, wa, wb, wgt, gb, tm=nctx)
        convo_l, qkv_l, p_l, pt_l = _inproj(xl, mod_l, None, gn1, wa, wb, wgt, gb, tm=512)
        zero_state = (jnp.zeros((bsz, 2 * N_HEADS, HEAD_DIM, HEAD_DIM), F32),
                      jnp.zeros((bsz, 2 * N_HEADS, 1, LANES), F32),
                      jnp.zeros((bsz, 2 * N_HEADS, 1, LANES), F32))
        hf_c, hb_c, state = _mlstm(qkv_c, p_c, pt_c, zero_state)
        hf_l, hb_l, _ = _mlstm(qkv_l, p_l, pt_l, state)

        xl, h2_l, lt_l = _mixout(convo_l, hf_l, hb_l, xl, mod_l, None, cw, cb, gh, wo, gn2, wr,
                                 tm=512, row_len=GRID_W)
        xl = _moe(lt_l, h2_l, xl, mod_l, None, gfin, wge, wue, wde, final=last)
        if not last:
            xc, h2_c, lt_c = _mixout(convo_c, hf_c, hb_c, xc, mod_l, ctx_row, cw, cb, gh, wo, gn2, wr,
                                     tm=nctx, row_len=nctx)
            xc = _moe(lt_c, h2_c, xc, mod_l, ctx_row, gfin, wge, wue, wde, final=False)
    return xl
```

```python
import functools

import jax
import jax.numpy as jnp
from jax import lax
from jax.experimental import pallas as pl
from jax.experimental.pallas import tpu as pltpu

EPS = 1e-6
N_HEADS = 4
HEAD_DIM = 128
CHUNK = 128
N_EXPERTS = 16
CAPACITY_FACTOR = 2
N_MOD = 6
GRID_W = 64
LANES = 128
ROUTE_TILE = 256
SLOT_CHUNK = 64
FIRST_CHUNK = 128
SLOT_PAD = 128
VMEM_LIMIT = 56 * 1024 * 1024

F32 = jnp.float32
BF16 = jnp.bfloat16
HIGHEST = lax.Precision.HIGHEST


def _params(sem, vmem=VMEM_LIMIT):
    return pltpu.CompilerParams(dimension_semantics=sem, vmem_limit_bytes=vmem)


def _sigmoid(x):
    return 1.0 / (1.0 + jnp.exp(-x))


def _mod_kernel(c_ref, w_ref, b_ref, o_ref):
    c = c_ref[...]
    s = c * _sigmoid(c)
    o_ref[0] = jnp.dot(s, w_ref[0], precision=HIGHEST, preferred_element_type=F32) + b_ref[0]


def _modulation(cc, w_mod, b_mod):
    depth, d, nm = w_mod.shape
    rows = cc.shape[0]
    tn = 1536
    return pl.pallas_call(
        _mod_kernel,
        out_shape=jax.ShapeDtypeStruct((depth, rows, nm), F32),
        grid=(depth, nm // tn),
        in_specs=[pl.BlockSpec((rows, d), lambda l, j: (0, 0)),
                  pl.BlockSpec((1, d, tn), lambda l, j: (l, 0, j)),
                  pl.BlockSpec((1, 1, tn), lambda l, j: (l, 0, j))],
        out_specs=pl.BlockSpec((1, rows, tn), lambda l, j: (l, 0, j)),
        compiler_params=_params(("arbitrary", "arbitrary")),
        name="modulation",
    )(cc, w_mod, b_mod.reshape(depth, 1, nm))


def _inproj_kernel(x_ref, mod_ref, gn_ref, wa_ref, wb_ref, wg_ref, gb_ref,
                   convo_ref, qk_ref, vt_ref, p_ref, pt_ref, *, tm):
    x = x_ref[0]
    m = mod_ref[0]
    ms = jnp.mean(x * x, axis=-1, keepdims=True)
    h = (x * lax.rsqrt(ms + EPS) * gn_ref[...]) * (1.0 + m[1:2]) + m[0:1]
    hb = h.astype(BF16)
    cw = 512
    for c in range(wa_ref.shape[1] // cw):
        convo_ref[0, :, c * cw:(c + 1) * cw] = jnp.dot(
            hb, wa_ref[:, c * cw:(c + 1) * cw], preferred_element_type=F32).astype(BF16)
    q = jnp.dot(hb, wb_ref[:, 0:cw], preferred_element_type=F32) * (HEAD_DIM ** -0.5)
    qk_ref[0, :, 0:cw] = q.astype(BF16)
    qk_ref[0, :, cw:2 * cw] = jnp.dot(hb, wb_ref[:, cw:2 * cw], preferred_element_type=F32).astype(BF16)
    v = jnp.dot(hb, wb_ref[:, 2 * cw:3 * cw], preferred_element_type=F32)
    vt_ref[0] = v.T.astype(BF16)

    g = jnp.dot(hb, wg_ref[...], preferred_element_type=F32) + gb_ref[...]
    kind = lax.broadcasted_iota(jnp.int32, (1, LANES), 1) // N_HEADS
    logf = jnp.minimum(g, 0.0) - jnp.log1p(jnp.exp(-jnp.abs(g)))
    g = jnp.where((kind == 1) | (kind == 3), logf, g)
    row = lax.broadcasted_iota(jnp.int32, (CHUNK, CHUNK), 0)
    col = lax.broadcasted_iota(jnp.int32, (CHUNK, CHUNK), 1)
    tril = jnp.where(col <= row, 1.0, 0.0).astype(F32)
    triu = jnp.where(col >= row, 1.0, 0.0).astype(F32)
    for r in range(tm // CHUNK):
        gc = g[r * CHUNK:(r + 1) * CHUNK]
        cum = jnp.dot(tril, gc, precision=HIGHEST, preferred_element_type=F32)
        suf = jnp.dot(triu, gc, precision=HIGHEST, preferred_element_type=F32)
        p = jnp.where(kind == 1, cum, jnp.where(kind == 3, suf, gc))
        pt_ref[0, :, r * CHUNK:(r + 1) * CHUNK] = p.T[:4 * N_HEADS, :]
        diff = p - pltpu.roll(p, LANES - N_HEADS, axis=1)
        p_ref[0, r * CHUNK:(r + 1) * CHUNK, :] = diff[:, :4 * N_HEADS]


def _inproj(x, mod_l, mod_row, gn, wa, wb, wg, gb, tm):
    b, n, d = x.shape
    per_sample = mod_row is None
    mod_map = (lambda bi, i: (bi, 0, 0)) if per_sample else (lambda bi, i: (mod_row, 0, 0))
    ng = 4 * N_HEADS
    dm = N_HEADS * HEAD_DIM
    return pl.pallas_call(
        functools.partial(_inproj_kernel, tm=tm),
        out_shape=(jax.ShapeDtypeStruct((b, n, wa.shape[1]), BF16),
                   jax.ShapeDtypeStruct((b, n, 2 * dm), BF16),
                   jax.ShapeDtypeStruct((b, dm, n), BF16),
                   jax.ShapeDtypeStruct((b, n, ng), F32),
                   jax.ShapeDtypeStruct((b, ng, n), F32)),
        grid=(b, n // tm),
        in_specs=[pl.BlockSpec((1, tm, d), lambda bi, i: (bi, i, 0)),
                  pl.BlockSpec((1, N_MOD, d), mod_map),
                  pl.BlockSpec((1, d), lambda bi, i: (0, 0)),
                  pl.BlockSpec(wa.shape, lambda bi, i: (0, 0)),
                  pl.BlockSpec(wb.shape, lambda bi, i: (0, 0)),
                  pl.BlockSpec(wg.shape, lambda bi, i: (0, 0)),
                  pl.BlockSpec((1, LANES), lambda bi, i: (0, 0))],
        out_specs=(pl.BlockSpec((1, tm, wa.shape[1]), lambda bi, i: (bi, i, 0)),
                   pl.BlockSpec((1, tm, 2 * dm), lambda bi, i: (bi, i, 0)),
                   pl.BlockSpec((1, dm, tm), lambda bi, i: (bi, 0, i)),
                   pl.BlockSpec((1, tm, ng), lambda bi, i: (bi, i, 0)),
                   pl.BlockSpec((1, ng, tm), lambda bi, i: (bi, 0, i))),
        compiler_params=_params(("arbitrary", "arbitrary")),
        name="inproj",
    )(x, mod_l, gn, wa, wb, wg, gb)


def _mlstm_kernel(qf_ref, qb_ref, vf_ref, vb_ref, pf_ref, pb_ref, ptf_ref, ptb_ref, c0_ref, n0_ref, m0_ref,
                  hf_ref, hb_ref, c1_ref, n1_ref, m1_ref, c_sc, n_sc, m_sc):
    j = pl.program_id(1)
    nc = pl.num_programs(1)
    L = CHUNK
    dm = N_HEADS * HEAD_DIM
    pad = 16

    @pl.when(j == 0)
    def _():
        c_sc[...] = c0_ref[0]
        n_sc[...] = n0_ref[0]
        m_sc[...] = m0_ref[0]

    W = 2 * HEAD_DIM
    row = lax.broadcasted_iota(jnp.int32, (L, W), 0)
    col = lax.broadcasted_iota(jnp.int32, (L, W), 1) % L
    nt_dims = (((1,), (1,)), ((), ()))
    zeros = jnp.zeros((L, HEAD_DIM), BF16)
    pairs = [(d, pr) for d in range(2) for pr in range(N_HEADS // 2)]

    def blockdiag(x):
        return jnp.concatenate([jnp.concatenate([x[:, :HEAD_DIM], zeros], axis=1),
                                jnp.concatenate([zeros, x[:, HEAD_DIM:]], axis=1)], axis=0)

    def per_head(x, lane):
        return jnp.concatenate([jnp.broadcast_to(x[:, lane:lane + 1], (1, HEAD_DIM)),
                                jnp.broadcast_to(x[:, HEAD_DIM + lane:HEAD_DIM + lane + 1], (1, HEAD_DIM))], axis=1)

    ks, vts, cts, nvs, r1 = [], [], [], [], []
    for d, pr in pairs:
        idx = d * (N_HEADS // 2) + pr
        qk_ref = (qf_ref, qb_ref)[d]
        q = qk_ref[0, :, pr * W:(pr + 1) * W]
        k = qk_ref[0, :, dm + pr * W:dm + (pr + 1) * W]
        ct, nvec = c_sc[idx], n_sc[idx]
        lhs = jnp.concatenate([k, ct.astype(BF16), jnp.broadcast_to(nvec, (pad, W)).astype(BF16)], axis=0)
        r1.append(lax.dot_general(lhs, blockdiag(q), nt_dims, preferred_element_type=F32))
        v_ref = (vf_ref, vb_ref)[d]
        vts.append(jnp.concatenate([v_ref[0, (2 * pr) * HEAD_DIM:(2 * pr + 1) * HEAD_DIM, :],
                                    v_ref[0, (2 * pr + 1) * HEAD_DIM:(2 * pr + 2) * HEAD_DIM, :]], axis=1))
        ks.append(k), cts.append(ct), nvs.append(nvec)

    w_intra, w_inter, floor, w_state, m_news, r3 = [], [], [], [], [], []
    for c, (d, pr) in enumerate(pairs):
        idx = d * (N_HEADS // 2) + pr
        ci = 2 * d * N_HEADS + 2 * pr
        p = (pf_ref, pb_ref)[d][0]
        pt = (ptf_ref, ptb_ref)[d][0]
        g_col = jnp.concatenate([jnp.broadcast_to(p[:, ci:ci + 1], (L, HEAD_DIM)),
                                 jnp.broadcast_to(p[:, ci + 1:ci + 2], (L, HEAD_DIM))], axis=1)
        i_row = jnp.concatenate([pt[ci:ci + 1, :], pt[ci + 1:ci + 2, :]], axis=1)
        b_row = jnp.concatenate([pt[ci + N_HEADS:ci + N_HEADS + 1, :], pt[ci + N_HEADS + 1:ci + N_HEADS + 2, :]],
                                axis=1)
        m_prev = m_sc[idx]
        mask = (row <= col) if d == 0 else (row >= col)
        last = L - 1 if d == 0 else 0
        dmat = jnp.where(mask, b_row + g_col, -jnp.inf)
        a = b_row + m_prev
        m_t = jnp.maximum(a, jnp.max(dmat, axis=0, keepdims=True))
        w_intra.append(jnp.exp(dmat - m_t))
        w_inter.append(jnp.exp(a - m_t))
        floor.append(jnp.exp(-m_t))
        m_new = per_head(m_t, last)
        b_last = per_head(b_row, last)
        w_state.append(jnp.exp(b_last + m_prev - m_new))
        m_news.append(m_new)
        w_tok = jnp.exp(b_last - b_row + i_row - m_new)
        lhs = jnp.concatenate([(vts[c].astype(F32) * w_tok).astype(BF16),
                               jnp.broadcast_to(w_tok, (pad, W)).astype(BF16)], axis=0)
        r3.append(jnp.dot(lhs, blockdiag(ks[c]), preferred_element_type=F32))

    for c, (d, pr) in enumerate(pairs):
        s = r1[c][0:L] * w_intra[c]
        den = w_inter[c] * r1[c][2 * L:2 * L + 1] + jnp.sum(s, axis=0, keepdims=True)
        inv = 1.0 / jnp.maximum(jnp.abs(den), floor[c])
        ht = (jnp.dot(vts[c], blockdiag(s.astype(BF16)), preferred_element_type=F32)
              + r1[c][L:2 * L] * w_inter[c]) * inv
        out_ref = (hf_ref, hb_ref)[d]
        out_ref[0, (2 * pr) * HEAD_DIM:(2 * pr + 1) * HEAD_DIM, :] = ht[:, :HEAD_DIM]
        out_ref[0, (2 * pr + 1) * HEAD_DIM:(2 * pr + 2) * HEAD_DIM, :] = ht[:, HEAD_DIM:]

    for c, (d, pr) in enumerate(pairs):
        idx = d * (N_HEADS // 2) + pr
        c_sc[idx] = w_state[c] * cts[c] + r3[c][0:HEAD_DIM]
        n_sc[idx] = w_state[c] * nvs[c] + r3[c][HEAD_DIM:HEAD_DIM + 1]
        m_sc[idx] = m_news[c]

    @pl.when(j == nc - 1)
    def _():
        c1_ref[0] = c_sc[...]
        n1_ref[0] = n_sc[...]
        m1_ref[0] = m_sc[...]


def _mlstm_zero_state(bsz):
    npair = N_HEADS
    return (jnp.zeros((bsz, npair, HEAD_DIM, 2 * HEAD_DIM), F32),
            jnp.zeros((bsz, npair, 1, 2 * HEAD_DIM), F32),
            jnp.zeros((bsz, npair, 1, 2 * HEAD_DIM), F32))


def _mlstm(qk, vt, p, pt, state):
    b, n, w = qk.shape
    nc = n // CHUNK
    ng = 4 * N_HEADS
    dm = N_HEADS * HEAD_DIM
    c0, n0, m0 = state
    fwd = lambda bi, j: (bi, j, 0)
    bwd = lambda bi, j: (bi, nc - 1 - j, 0)
    fwd_t = lambda bi, j: (bi, 0, j)
    bwd_t = lambda bi, j: (bi, 0, nc - 1 - j)
    st4 = lambda bi, j: (bi, 0, 0, 0)
    outs = pl.pallas_call(
        _mlstm_kernel,
        out_shape=(jax.ShapeDtypeStruct((b, dm, n), F32),
                   jax.ShapeDtypeStruct((b, dm, n), F32),
                   jax.ShapeDtypeStruct(c0.shape, F32),
                   jax.ShapeDtypeStruct(n0.shape, F32),
                   jax.ShapeDtypeStruct(m0.shape, F32)),
        grid=(b, nc),
        in_specs=[pl.BlockSpec((1, CHUNK, w), fwd),
                  pl.BlockSpec((1, CHUNK, w), bwd),
                  pl.BlockSpec((1, dm, CHUNK), fwd_t),
                  pl.BlockSpec((1, dm, CHUNK), bwd_t),
                  pl.BlockSpec((1, CHUNK, ng), fwd),
                  pl.BlockSpec((1, CHUNK, ng), bwd),
                  pl.BlockSpec((1, ng, CHUNK), fwd_t),
                  pl.BlockSpec((1, ng, CHUNK), bwd_t),
                  pl.BlockSpec((1,) + c0.shape[1:], st4),
                  pl.BlockSpec((1,) + n0.shape[1:], st4),
                  pl.BlockSpec((1,) + m0.shape[1:], st4)],
        out_specs=(pl.BlockSpec((1, dm, CHUNK), fwd_t),
                   pl.BlockSpec((1, dm, CHUNK), bwd_t),
                   pl.BlockSpec((1,) + c0.shape[1:], st4),
                   pl.BlockSpec((1,) + n0.shape[1:], st4),
                   pl.BlockSpec((1,) + m0.shape[1:], st4)),
        scratch_shapes=[pltpu.VMEM(c0.shape[1:], F32),
                        pltpu.VMEM(n0.shape[1:], F32),
                        pltpu.VMEM(m0.shape[1:], F32)],
        compiler_params=_params(("arbitrary", "arbitrary")),
        name="mlstm",
    )(qk, qk, vt, vt, p, p, pt, pt, c0, n0, m0)
    return outs[0], outs[1], (outs[2], outs[3], outs[4])


def _mixout_kernel(convo_ref, hf_ref, hb_ref, x_ref, mod_ref, cw_ref, cb_ref, gh_ref, wo_ref, gn2_ref, wr_ref,
                   xo_ref, h2_ref, lt_ref, *, tm, row_len):
    dc = cw_ref.shape[1]
    cv = convo_ref[0]
    bg = cv[:, 0:dc].astype(F32)
    cg = cv[:, dc:2 * dc].astype(F32)
    xi = cv[:, 2 * dc:3 * dc].astype(F32)
    og = cv[:, 3 * dc:].astype(F32)
    u = cg * xi
    t = lax.broadcasted_iota(jnp.int32, (tm, dc), 0) % row_len
    u_prev = jnp.where(t == 0, 0.0, pltpu.roll(u, 1, axis=0))
    u_next = jnp.where(t == row_len - 1, 0.0, pltpu.roll(u, tm - 1, axis=0))
    cw = cw_ref[...]
    yc = bg * (cw[0:1] * u_prev + cw[1:2] * u + cw[2:3] * u_next + cb_ref[...])

    hm = hf_ref[0] + hb_ref[0]
    parts = []
    for hd in range(N_HEADS):
        hh = hm[hd * HEAD_DIM:(hd + 1) * HEAD_DIM, :]
        parts.append(hh * lax.rsqrt(jnp.mean(hh * hh, axis=0, keepdims=True) + EPS))
    ym = _sigmoid(og) * (jnp.concatenate(parts, axis=0).T * gh_ref[...])

    cat = jnp.concatenate([yc, ym], axis=-1).astype(BF16)
    y = jnp.dot(cat, wo_ref[...], preferred_element_type=F32)
    m = mod_ref[0]
    xn = x_ref[0] + m[2:3] * y
    xo_ref[0] = xn
    ms = jnp.mean(xn * xn, axis=-1, keepdims=True)
    h2 = (xn * lax.rsqrt(ms + EPS) * gn2_ref[...]) * (1.0 + m[4:5]) + m[3:4]
    h2_ref[0] = h2.astype(BF16)
    lg = jnp.dot(h2, wr_ref[...], precision=HIGHEST, preferred_element_type=F32)
    for r in range(tm // LANES):
        lt_ref[0, :, r * LANES:(r + 1) * LANES] = lg[r * LANES:(r + 1) * LANES, :].T[:N_EXPERTS, :]


def _mixout(convo, hf, hb, x, mod_l, mod_row, cw, cb, gh, wo, gn2, wr, tm, row_len):
    b, n, d = x.shape
    per_sample = mod_row is None
    mod_map = (lambda bi, i: (bi, 0, 0)) if per_sample else (lambda bi, i: (mod_row, 0, 0))
    tok = lambda bi, i: (bi, i, 0)
    cst = lambda bi, i: (0, 0)
    return pl.pallas_call(
        functools.partial(_mixout_kernel, tm=tm, row_len=row_len),
        out_shape=(jax.ShapeDtypeStruct((b, n, d), F32),
                   jax.ShapeDtypeStruct((b, n, d), BF16),
                   jax.ShapeDtypeStruct((b, N_EXPERTS, n), F32)),
        grid=(b, n // tm),
        in_specs=[pl.BlockSpec((1, tm, convo.shape[2]), tok),
                  pl.BlockSpec((1, hf.shape[1], tm), lambda bi, i: (bi, 0, i)),
                  pl.BlockSpec((1, hb.shape[1], tm), lambda bi, i: (bi, 0, i)),
                  pl.BlockSpec((1, tm, d), tok),
                  pl.BlockSpec((1, N_MOD, d), mod_map),
                  pl.BlockSpec(cw.shape, cst),
                  pl.BlockSpec(cb.shape, cst),
                  pl.BlockSpec(gh.shape, cst),
                  pl.BlockSpec(wo.shape, cst),
                  pl.BlockSpec(gn2.shape, cst),
                  pl.BlockSpec(wr.shape, cst)],
        out_specs=(pl.BlockSpec((1, tm, d), tok),
                   pl.BlockSpec((1, tm, d), tok),
                   pl.BlockSpec((1, N_EXPERTS, tm), lambda bi, i: (bi, 0, i))),
        compiler_params=_params(("arbitrary", "arbitrary")),
        name="mixout",
    )(convo, hf, hb, x, mod_l, cw, cb, gh, wo, gn2, wr)


def _route_kernel(lt_ref, pos_ref, g_ref, cs_ref, aff_sc, sel_sc, *, n, cap):
    lg = lt_ref[0]
    e = jnp.exp(lg - jnp.max(lg, axis=0, keepdims=True))
    aff_sc[...] = e / jnp.sum(e, axis=0, keepdims=True)
    capf = jnp.float32(cap)

    def count_ge(cand):
        acc = jnp.zeros((N_EXPERTS, LANES), F32)
        for j in range(n // LANES):
            acc = acc + jnp.where(aff_sc[:, j * LANES:(j + 1) * LANES] >= cand, 1.0, 0.0)
        return jnp.sum(acc, axis=1, keepdims=True)

    def exp_step(_, c):
        elo, ehi = c
        emid = jnp.floor((elo + ehi) * 0.5)
        ok = count_ge(jnp.exp2(emid)) >= capf
        return jnp.where(ok, emid, elo), jnp.where(ok, ehi, emid)

    elo, ehi = lax.fori_loop(0, 8, exp_step, (jnp.full((N_EXPERTS, 1), -128.0, F32), jnp.ones((N_EXPERTS, 1), F32)))

    def val_step(_, c):
        lo, hi = c
        mid = lo + (hi - lo) * 0.5
        ok = count_ge(mid) >= capf
        return jnp.where(ok, mid, lo), jnp.where(ok, hi, mid)

    lo, hi = lax.fori_loop(0, 40, val_step, (jnp.where(elo <= -128.0, 0.0, jnp.exp2(elo)), jnp.exp2(ehi)))
    need = capf - count_ge(hi)

    row = lax.broadcasted_iota(jnp.int32, (LANES, LANES), 0)
    col = lax.broadcasted_iota(jnp.int32, (LANES, LANES), 1)
    upper = jnp.where(row < col, 1.0, 0.0).astype(BF16)
    lane = lax.broadcasted_iota(jnp.int32, (N_EXPERTS, LANES), 1)
    per_tile = ROUTE_TILE // LANES

    ceq = jnp.zeros((N_EXPERTS, 1), F32)
    for c in range(n // LANES):
        a = aff_sc[:, c * LANES:(c + 1) * LANES]
        tie = (a >= lo) & (a < hi)
        eq = jnp.where(tie, 1.0, 0.0)
        peq = jnp.dot(eq.astype(BF16), upper, preferred_element_type=F32) + ceq
        sel = (a >= hi) | (tie & (peq < need))
        g_ref[0, :, c * LANES:(c + 1) * LANES] = jnp.where(sel, a, 0.0)
        sel_sc[:, c * LANES:(c + 1) * LANES] = jnp.where(sel, 1.0, 0.0)
        ceq = ceq + jnp.sum(eq, axis=1, keepdims=True)

    csel = jnp.zeros((N_EXPERTS, 1), F32)
    offs = jnp.zeros((N_EXPERTS, LANES), jnp.int32)
    for c in range(n // LANES):
        self = sel_sc[:, c * LANES:(c + 1) * LANES]
        psel = jnp.dot(self.astype(BF16), upper, preferred_element_type=F32) + csel
        pos_ref[0, :, c * LANES:(c + 1) * LANES] = jnp.where(self > 0.0, psel.astype(jnp.int32), -1)
        if c % per_tile == 0:
            offs = jnp.where(lane == c // per_tile, csel.astype(jnp.int32), offs)
        csel = csel + jnp.sum(self, axis=1, keepdims=True)
    cs_ref[0] = jnp.where(lane >= n // ROUTE_TILE, csel.astype(jnp.int32), offs)


def _route(lt):
    b, ne, n = lt.shape
    cap = CAPACITY_FACTOR * n // N_EXPERTS
    blk = pl.BlockSpec((1, ne, n), lambda bi: (bi, 0, 0))
    return pl.pallas_call(
        functools.partial(_route_kernel, n=n, cap=cap),
        out_shape=(jax.ShapeDtypeStruct((b, ne, n), jnp.int32),
                   jax.ShapeDtypeStruct((b, ne, n), F32),
                   jax.ShapeDtypeStruct((b, ne, LANES), jnp.int32)),
        grid=(b,),
        in_specs=[blk],
        out_specs=(blk, blk, pl.BlockSpec((1, ne, LANES), lambda bi: (bi, 0, 0))),
        scratch_shapes=[pltpu.VMEM((ne, n), F32), pltpu.VMEM((ne, n), F32)],
        compiler_params=_params(("arbitrary",)),
        name="route",
    )(lt)


def _ffn_kernel(cs_ref, h2_ref, pos_ref, wg_ref, wu_ref, wd_ref, ye_ref, xe_sc, *, tdma, nt_pad, cap, row_chunk):
    e = pl.program_id(0)
    b = pl.program_id(1)
    s = pl.program_id(2)
    ns = pl.num_programs(2)
    sub = tdma // ROUTE_TILE

    @pl.when(s == 0)
    def _():
        xe_sc[...] = jnp.zeros_like(xe_sc)

    cs_base = (b * N_EXPERTS + e) * nt_pad + s * sub

    jrow1 = lax.broadcasted_iota(jnp.int32, (FIRST_CHUNK, ROUTE_TILE), 0)
    for i in range(sub):
        base = pl.multiple_of((cs_ref[cs_base + i] // 8) * 8, 8)
        prow = pos_ref[0, 0, :, i * ROUTE_TILE:(i + 1) * ROUTE_TILE]
        xt = h2_ref[0, i * ROUTE_TILE:(i + 1) * ROUTE_TILE, :]
        onehot = jnp.where(prow == base + jrow1, 1.0, 0.0).astype(BF16)
        xe_sc[pl.ds(base, FIRST_CHUNK), :] += jnp.dot(onehot, xt, preferred_element_type=F32)

    jrow = lax.broadcasted_iota(jnp.int32, (SLOT_CHUNK, ROUTE_TILE), 0)

    def tile_body(i, carry):
        first_end = (cs_ref[cs_base + i] // 8) * 8 + FIRST_CHUNK
        c1 = cs_ref[cs_base + i + 1]
        nch = jnp.where(c1 > first_end, (c1 - first_end + SLOT_CHUNK - 1) // SLOT_CHUNK, 0)
        off = pl.multiple_of(i * ROUTE_TILE, ROUTE_TILE)

        def chunk(r, c):
            rb = pl.multiple_of(first_end + r * SLOT_CHUNK, 8)
            prow = pos_ref[0, 0, :, pl.ds(off, ROUTE_TILE)]
            onehot = jnp.where(prow == rb + jrow, 1.0, 0.0).astype(BF16)
            xe_sc[pl.ds(rb, SLOT_CHUNK), :] += jnp.dot(onehot, h2_ref[0, pl.ds(off, ROUTE_TILE), :],
                                                       preferred_element_type=F32)
            return c

        return lax.fori_loop(0, nch, chunk, carry)

    lax.fori_loop(0, sub, tile_body, 0)

    @pl.when(s == ns - 1)
    def _():
        for rc in range(cap // row_chunk):
            rows = slice(rc * row_chunk, (rc + 1) * row_chunk)
            xb = xe_sc[rows, :].astype(BF16)
            a = jnp.dot(xb, wg_ref[0], preferred_element_type=F32)
            u = jnp.dot(xb, wu_ref[0], preferred_element_type=F32)
            hm = (a * _sigmoid(a) * u).astype(BF16)
            ye_ref[0, 0, rows, :] = jnp.dot(hm, wd_ref[0], preferred_element_type=F32).astype(BF16)
        ye_ref[0, 0, cap:, :] = jnp.zeros((ye_ref.shape[2] - cap, ye_ref.shape[3]), BF16)


def _ffn(cs_flat, nt_pad, h2, pos4, wg, wu, wd):
    b, n, d = h2.shape
    ne, _, f = wg.shape
    cap = CAPACITY_FACTOR * n // N_EXPERTS
    rows = cap + SLOT_PAD
    tdma = min(n, 2048)
    row_chunk = min(cap, 512)
    grid_spec = pltpu.PrefetchScalarGridSpec(
        num_scalar_prefetch=1,
        grid=(ne, b, n // tdma),
        in_specs=[pl.BlockSpec((1, tdma, d), lambda e, bi, s, cs: (bi, s, 0)),
                  pl.BlockSpec((1, 1, 1, tdma), lambda e, bi, s, cs: (bi, e, 0, s)),
                  pl.BlockSpec((1, d, f), lambda e, bi, s, cs: (e, 0, 0)),
                  pl.BlockSpec((1, d, f), lambda e, bi, s, cs: (e, 0, 0)),
                  pl.BlockSpec((1, f, d), lambda e, bi, s, cs: (e, 0, 0))],
        out_specs=pl.BlockSpec((1, 1, rows, d), lambda e, bi, s, cs: (bi, e, 0, 0)),
        scratch_shapes=[pltpu.VMEM((rows, d), F32)])
    return pl.pallas_call(
        functools.partial(_ffn_kernel, tdma=tdma, nt_pad=nt_pad, cap=cap, row_chunk=row_chunk),
        out_shape=jax.ShapeDtypeStruct((b, ne, rows, d), BF16),
        grid_spec=grid_spec,
        compiler_params=_params(("arbitrary", "arbitrary", "arbitrary")),
        name="ffn",
    )(cs_flat, h2, pos4, wg, wu, wd)


def _combine_kernel(cs_ref, pos_ref, g_ref, x_ref, mod_ref, gfin_ref, ye_hbm, out_ref,
                    ystage, pt_sc, acc_sc, sem, *, nt_pad, final):
    b = pl.program_id(0)
    i = pl.program_id(1)
    nt = pl.num_programs(1)
    step = b * nt + i
    slot = step % 2
    group = 4 * SLOT_CHUNK

    @pl.when(step == 0)
    def _():
        ystage[...] = jnp.zeros_like(ystage)

    def chunk_plan(bb, ii, e):
        c0 = cs_ref[(bb * N_EXPERTS + e) * nt_pad + ii]
        c1 = cs_ref[(bb * N_EXPERTS + e) * nt_pad + ii + 1]
        base = (c0 // 16) * 16
        return base, jnp.where(c1 > c0, (c1 - base + SLOT_CHUNK - 1) // SLOT_CHUNK, 0)

    def chunk_copy(bb, e, rb, sl, k):
        return pltpu.make_async_copy(ye_hbm.at[bb, e, pl.ds(rb, SLOT_CHUNK), :],
                                     ystage.at[sl, pl.ds(k * SLOT_CHUNK, SLOT_CHUNK), :], sem.at[sl])

    def issue(bb, ii, sl):
        k = jnp.int32(0)
        for e in range(N_EXPERTS):
            base, nch = chunk_plan(bb, ii, e)

            def start(r, kk, e=e, base=base):
                chunk_copy(bb, e, pl.multiple_of(base + r * SLOT_CHUNK, 16), sl, kk).start()
                return kk + 1

            k = lax.fori_loop(0, nch, start, k)

    @pl.when(step == 0)
    def _():
        issue(b, i, slot)

    @pl.when(step + 1 < pl.num_programs(0) * nt)
    def _():
        wrap = i + 1 == nt
        issue(jnp.where(wrap, b + 1, b), jnp.where(wrap, 0, i + 1), 1 - slot)

    jrow = lax.broadcasted_iota(jnp.int32, (SLOT_CHUNK, ROUTE_TILE), 0)
    k = jnp.int32(0)
    for e in range(N_EXPERTS):
        base, nch = chunk_plan(b, i, e)
        prow = pos_ref[0, e:e + 1, :]
        grow = g_ref[0, e:e + 1, :]

        def weights(r, kk, base=base, prow=prow, grow=grow):
            row0 = pl.multiple_of(kk * SLOT_CHUNK, SLOT_CHUNK)
            pt_sc[pl.ds(row0, SLOT_CHUNK), :] = jnp.where(prow == base + r * SLOT_CHUNK + jrow, grow, 0.0)
            return kk + 1

        k = lax.fori_loop(0, nch, weights, k)

    kpad = ((k + 3) // 4) * 4

    def zero_pad(kk, c):
        row0 = pl.multiple_of(kk * SLOT_CHUNK, SLOT_CHUNK)
        pt_sc[pl.ds(row0, SLOT_CHUNK), :] = jnp.zeros((SLOT_CHUNK, ROUTE_TILE), F32)
        return c

    lax.fori_loop(k, kpad, zero_pad, 0)

    def wait_one(kk, c):
        chunk_copy(0, 0, 0, slot, kk).wait()
        return c

    lax.fori_loop(0, k, wait_one, 0)

    acc_sc[...] = jnp.zeros_like(acc_sc)

    def matmul_group(gi, c):
        row0 = pl.multiple_of(gi * group, group)
        w = pt_sc[pl.ds(row0, group), :].T.astype(BF16)
        acc_sc[...] += jnp.dot(w, ystage[slot, pl.ds(row0, group), :], preferred_element_type=F32)
        return c

    lax.fori_loop(0, kpad // 4, matmul_group, 0)

    xn = x_ref[0] + mod_ref[0][5:6] * acc_sc[...]
    if final:
        ms = jnp.mean(xn * xn, axis=-1, keepdims=True)
        xn = xn * lax.rsqrt(ms + EPS) * gfin_ref[...]
    out_ref[0] = xn


def _combine(cs_flat, nt_pad, pos, g, x, mod_l, mod_row, gfin, ye, final):
    b, n, d = x.shape
    per_sample = mod_row is None
    mod_map = ((lambda bi, i, cs: (bi, 0, 0)) if per_sample else (lambda bi, i, cs: (mod_row, 0, 0)))
    max_chunks = N_EXPERTS * ((ROUTE_TILE + 15 + SLOT_CHUNK - 1) // SLOT_CHUNK) + 4
    grid_spec = pltpu.PrefetchScalarGridSpec(
        num_scalar_prefetch=1,
        grid=(b, n // ROUTE_TILE),
        in_specs=[pl.BlockSpec((1, N_EXPERTS, ROUTE_TILE), lambda bi, i, cs: (bi, 0, i)),
                  pl.BlockSpec((1, N_EXPERTS, ROUTE_TILE), lambda bi, i, cs: (bi, 0, i)),
                  pl.BlockSpec((1, ROUTE_TILE, d), lambda bi, i, cs: (bi, i, 0)),
                  pl.BlockSpec((1, N_MOD, d), mod_map),
                  pl.BlockSpec((1, d), lambda bi, i, cs: (0, 0)),
                  pl.BlockSpec(memory_space=pl.ANY)],
        out_specs=pl.BlockSpec((1, ROUTE_TILE, d), lambda bi, i, cs: (bi, i, 0)),
        scratch_shapes=[pltpu.VMEM((2, max_chunks * SLOT_CHUNK, d), BF16),
                        pltpu.VMEM((max_chunks * SLOT_CHUNK, ROUTE_TILE), F32),
                        pltpu.VMEM((ROUTE_TILE, d), F32),
                        pltpu.SemaphoreType.DMA((2,))])
    return pl.pallas_call(
        functools.partial(_combine_kernel, nt_pad=nt_pad, final=final),
        out_shape=jax.ShapeDtypeStruct((b, n, d), F32),
        grid_spec=grid_spec,
        compiler_params=_params(("arbitrary", "arbitrary")),
        name="combine",
    )(cs_flat, pos, g, x, mod_l, gfin, ye)


def _moe(lt, h2, x, mod_l, mod_row, gfin, wg, wu, wd, final):
    b, n, _ = x.shape
    pos, g, cs = _route(lt)
    nt_pad = ((n // ROUTE_TILE + 1 + 7) // 8) * 8
    cs_flat = cs[:, :, :nt_pad].reshape(-1)
    ye = _ffn(cs_flat, nt_pad, h2, pos.reshape(b, N_EXPERTS, 1, n), wg, wu, wd)
    return _combine(cs_flat, nt_pad, pos, g, x, mod_l, mod_row, gfin, ye, final)


def kernel(x, c, ctx, c_ctx, w_mod, b_mod, g_norm1, g_norm2, w_in, w_out, conv_w, conv_b, gate_b, g_head,
           w_router, w_gate_e, w_up_e, w_down_e, g_final):
    bsz, n, d = x.shape
    nctx = ctx.shape[1]
    depth = w_mod.shape[0]
    dm = N_HEADS * HEAD_DIM
    dc = conv_w.shape[2]
    ng = 4 * N_HEADS
    ctx_row = bsz

    rows = ((bsz + 1 + 7) // 8) * 8
    cc = jnp.zeros((rows, d), F32).at[:bsz].set(c).at[bsz].set(c_ctx)
    mod = _modulation(cc, w_mod, b_mod).reshape(depth, rows, N_MOD, d)
    gfin = g_final.reshape(1, d)

    xl, xc = x, ctx
    for layer in range(depth):
        last = layer == depth - 1
        mod_l = mod[layer]
        wi = w_in[layer]
        wa = jnp.concatenate([wi[:, :3 * dc], wi[:, 3 * dc + 3 * dm:3 * dc + 4 * dm]], axis=1).astype(BF16)
        wb = wi[:, 3 * dc:3 * dc + 3 * dm].astype(BF16)
        wgt = jnp.zeros((d, LANES), BF16).at[:, :ng].set(wi[:, 3 * dc + 4 * dm:].astype(BF16))
        gb = jnp.zeros((1, LANES), F32).at[0, :ng].set(gate_b[layer].reshape(-1))
        gn1 = g_norm1[layer].reshape(1, d)
        gn2 = g_norm2[layer].reshape(1, d)
        cw = jnp.zeros((8, dc), F32).at[:conv_w.shape[1]].set(conv_w[layer])
        cb = conv_b[layer].reshape(1, dc)
        gh = g_head[layer].reshape(1, dm)
        wo = w_out[layer].astype(BF16)
        wr = jnp.zeros((d, LANES), F32).at[:, :N_EXPERTS].set(w_router[layer])
        wge = w_gate_e[layer].astype(BF16)
        wue = w_up_e[layer].astype(BF16)
        wde = w_down_e[layer].astype(BF16)

        convo_c, qk_c, vt_c, p_c, pt_c = _inproj(xc, mod_l, ctx_row, gn1, wa, wb, wgt, gb, tm=nctx)
        convo_l, qk_l, vt_l, p_l, pt_l = _inproj(xl, mod_l, None, gn1, wa, wb, wgt, gb, tm=512)
        zero_state = _mlstm_zero_state(bsz)
        hf_c, hb_c, state = _mlstm(qk_c, vt_c, p_c, pt_c, zero_state)
        hf_l, hb_l, _ = _mlstm(qk_l, vt_l, p_l, pt_l, state)

        xl, h2_l, lt_l = _mixout(convo_l, hf_l, hb_l, xl, mod_l, None, cw, cb, gh, wo, gn2, wr,
                                 tm=512, row_len=GRID_W)
        xl = _moe(lt_l, h2_l, xl, mod_l, None, gfin, wge, wue, wde, final=last)
        if not last:
            xc, h2_c, lt_c = _mixout(convo_c, hf_c, hb_c, xc, mod_l, ctx_row, cw, cb, gh, wo, gn2, wr,
                                     tm=nctx, row_len=nctx)
            xc = _moe(lt_c, h2_c, xc, mod_l, ctx_row, gfin, wge, wue, wde, final=False)
    return xl
```

```python
import functools

import jax
import jax.numpy as jnp
from jax import lax
from jax.experimental import pallas as pl
from jax.experimental.pallas import tpu as pltpu

EPS = 1e-6
N_HEADS = 4
HEAD_DIM = 128
CHUNK = 128
MLSTM_CHUNKS_PER_STEP = 2
N_EXPERTS = 16
CAPACITY_FACTOR = 2
N_MOD = 6
GRID_W = 64
LANES = 128
ROUTE_TILE = 256
SLOT_CHUNK = 64
FIRST_CHUNK = 64
SLOT_PAD = 128
COMBINE_STATIC_CHUNKS = 28
VMEM_LIMIT = 56 * 1024 * 1024

F32 = jnp.float32
BF16 = jnp.bfloat16
HIGHEST = lax.Precision.HIGHEST


def _params(sem, vmem=VMEM_LIMIT):
    return pltpu.CompilerParams(dimension_semantics=sem, vmem_limit_bytes=vmem)


def _sigmoid(x):
    return 1.0 / (1.0 + jnp.exp(-x))


def _floor_to(x, m):
    return jnp.bitwise_and(x, -m)


def _ceil_div(x, m):
    return jnp.right_shift(x + (m - 1), m.bit_length() - 1)


def _mod_kernel(c_ref, w_ref, b_ref, o_ref):
    c = c_ref[...]
    s = c * _sigmoid(c)
    o_ref[0] = jnp.dot(s, w_ref[0], precision=HIGHEST, preferred_element_type=F32) + b_ref[0]


def _modulation(cc, w_mod, b_mod):
    depth, d, nm = w_mod.shape
    rows = cc.shape[0]
    tn = 1536
    return pl.pallas_call(
        _mod_kernel,
        out_shape=jax.ShapeDtypeStruct((depth, rows, nm), F32),
        grid=(depth, nm // tn),
        in_specs=[pl.BlockSpec((rows, d), lambda l, j: (0, 0)),
                  pl.BlockSpec((1, d, tn), lambda l, j: (l, 0, j)),
                  pl.BlockSpec((1, 1, tn), lambda l, j: (l, 0, j))],
        out_specs=pl.BlockSpec((1, rows, tn), lambda l, j: (l, 0, j)),
        compiler_params=_params(("arbitrary", "arbitrary")),
        name="modulation",
    )(cc, w_mod, b_mod.reshape(depth, 1, nm))


def _inproj_kernel(x_ref, mod_ref, gn_ref, wa_ref, wb_ref, wg_ref, gb_ref,
                   convo_ref, qk_ref, vt_ref, p_ref, pt_ref, *, tm):
    x = x_ref[0]
    m = mod_ref[0]
    ms = jnp.mean(x * x, axis=-1, keepdims=True)
    h = (x * lax.rsqrt(ms + EPS) * gn_ref[...]) * (1.0 + m[1:2]) + m[0:1]
    hb = h.astype(BF16)
    cw = 512
    for c in range(wa_ref.shape[1] // cw):
        convo_ref[0, :, c * cw:(c + 1) * cw] = jnp.dot(
            hb, wa_ref[:, c * cw:(c + 1) * cw], preferred_element_type=F32).astype(BF16)
    q = jnp.dot(hb, wb_ref[:, 0:cw], preferred_element_type=F32) * (HEAD_DIM ** -0.5)
    qk_ref[0, :, 0:cw] = q.astype(BF16)
    qk_ref[0, :, cw:2 * cw] = jnp.dot(hb, wb_ref[:, cw:2 * cw], preferred_element_type=F32).astype(BF16)
    v = jnp.dot(hb, wb_ref[:, 2 * cw:3 * cw], preferred_element_type=F32)
    vt_ref[0] = v.T.astype(BF16)

    g = jnp.dot(hb, wg_ref[...], preferred_element_type=F32) + gb_ref[...]
    kind = lax.broadcasted_iota(jnp.int32, (1, LANES), 1) // N_HEADS
    logf = jnp.minimum(g, 0.0) - jnp.log1p(jnp.exp(-jnp.abs(g)))
    g = jnp.where((kind == 1) | (kind == 3), logf, g)
    row = lax.broadcasted_iota(jnp.int32, (CHUNK, CHUNK), 0)
    col = lax.broadcasted_iota(jnp.int32, (CHUNK, CHUNK), 1)
    tril = jnp.where(col <= row, 1.0, 0.0).astype(F32)
    triu = jnp.where(col >= row, 1.0, 0.0).astype(F32)
    for r in range(tm // CHUNK):
        gc = g[r * CHUNK:(r + 1) * CHUNK]
        cum = jnp.dot(tril, gc, precision=HIGHEST, preferred_element_type=F32)
        suf = jnp.dot(triu, gc, precision=HIGHEST, preferred_element_type=F32)
        p = jnp.where(kind == 1, cum, jnp.where(kind == 3, suf, gc))
        pt_ref[0, :, r * CHUNK:(r + 1) * CHUNK] = p.T[:4 * N_HEADS, :]
        diff = p - pltpu.roll(p, LANES - N_HEADS, axis=1)
        p_ref[0, r * CHUNK:(r + 1) * CHUNK, :] = diff[:, :4 * N_HEADS]


def _inproj(x, mod_l, mod_row, gn, wa, wb, wg, gb, tm):
    b, n, d = x.shape
    per_sample = mod_row is None
    mod_map = (lambda bi, i: (bi, 0, 0)) if per_sample else (lambda bi, i: (mod_row, 0, 0))
    ng = 4 * N_HEADS
    dm = N_HEADS * HEAD_DIM
    return pl.pallas_call(
        functools.partial(_inproj_kernel, tm=tm),
        out_shape=(jax.ShapeDtypeStruct((b, n, wa.shape[1]), BF16),
                   jax.ShapeDtypeStruct((b, n, 2 * dm), BF16),
                   jax.ShapeDtypeStruct((b, dm, n), BF16),
                   jax.ShapeDtypeStruct((b, n, ng), F32),
                   jax.ShapeDtypeStruct((b, ng, n), F32)),
        grid=(b, n // tm),
        in_specs=[pl.BlockSpec((1, tm, d), lambda bi, i: (bi, i, 0)),
                  pl.BlockSpec((1, N_MOD, d), mod_map),
                  pl.BlockSpec((1, d), lambda bi, i: (0, 0)),
                  pl.BlockSpec(wa.shape, lambda bi, i: (0, 0)),
                  pl.BlockSpec(wb.shape, lambda bi, i: (0, 0)),
                  pl.BlockSpec(wg.shape, lambda bi, i: (0, 0)),
                  pl.BlockSpec((1, LANES), lambda bi, i: (0, 0))],
        out_specs=(pl.BlockSpec((1, tm, wa.shape[1]), lambda bi, i: (bi, i, 0)),
                   pl.BlockSpec((1, tm, 2 * dm), lambda bi, i: (bi, i, 0)),
                   pl.BlockSpec((1, dm, tm), lambda bi, i: (bi, 0, i)),
                   pl.BlockSpec((1, tm, ng), lambda bi, i: (bi, i, 0)),
                   pl.BlockSpec((1, ng, tm), lambda bi, i: (bi, 0, i))),
        compiler_params=_params(("arbitrary", "arbitrary")),
        name="inproj",
    )(x, mod_l, gn, wa, wb, wg, gb)


def _mlstm_kernel(qf_ref, qb_ref, vf_ref, vb_ref, pf_ref, pb_ref, ptf_ref, ptb_ref, c0_ref, n0_ref, m0_ref,
                  hf_ref, hb_ref, c1_ref, n1_ref, m1_ref, c_sc, n_sc, m_sc):
    j = pl.program_id(1)
    nc = pl.num_programs(1)
    L = CHUNK
    dm = N_HEADS * HEAD_DIM
    pad = 16

    @pl.when(j == 0)
    def _():
        c_sc[...] = c0_ref[0]
        n_sc[...] = n0_ref[0]
        m_sc[...] = m0_ref[0]

    W = 2 * HEAD_DIM
    row = lax.broadcasted_iota(jnp.int32, (L, W), 0)
    col = lax.broadcasted_iota(jnp.int32, (L, W), 1) % L
    nt_dims = (((1,), (1,)), ((), ()))
    zeros = jnp.zeros((L, HEAD_DIM), BF16)
    pairs = [(d, pr) for d in range(2) for pr in range(N_HEADS // 2)]

    def blockdiag(x):
        return jnp.concatenate([jnp.concatenate([x[:, :HEAD_DIM], zeros], axis=1),
                                jnp.concatenate([zeros, x[:, HEAD_DIM:]], axis=1)], axis=0)

    def per_head(x, lane):
        return jnp.concatenate([jnp.broadcast_to(x[:, lane:lane + 1], (1, HEAD_DIM)),
                                jnp.broadcast_to(x[:, HEAD_DIM + lane:HEAD_DIM + lane + 1], (1, HEAD_DIM))], axis=1)

    def chunk_step(sub_f, sub_b):
        tok = (slice(sub_f * L, (sub_f + 1) * L), slice(sub_b * L, (sub_b + 1) * L))
        ks, vts, cts, nvs, r1 = [], [], [], [], []
        for d, pr in pairs:
            idx = d * (N_HEADS // 2) + pr
            qk_ref = (qf_ref, qb_ref)[d]
            q = qk_ref[0, tok[d], pr * W:(pr + 1) * W]
            k = qk_ref[0, tok[d], dm + pr * W:dm + (pr + 1) * W]
            ct, nvec = c_sc[idx], n_sc[idx]
            lhs = jnp.concatenate([k, ct.astype(BF16), jnp.broadcast_to(nvec, (pad, W)).astype(BF16)], axis=0)
            r1.append(lax.dot_general(lhs, blockdiag(q), nt_dims, preferred_element_type=F32))
            v_ref = (vf_ref, vb_ref)[d]
            vts.append(jnp.concatenate([v_ref[0, (2 * pr) * HEAD_DIM:(2 * pr + 1) * HEAD_DIM, tok[d]],
                                        v_ref[0, (2 * pr + 1) * HEAD_DIM:(2 * pr + 2) * HEAD_DIM, tok[d]]], axis=1))
            ks.append(k), cts.append(ct), nvs.append(nvec)

        w_intra, w_inter, floor, w_state, m_news, r3 = [], [], [], [], [], []
        for c, (d, pr) in enumerate(pairs):
            idx = d * (N_HEADS // 2) + pr
            ci = 2 * d * N_HEADS + 2 * pr
            p = (pf_ref, pb_ref)[d][0, tok[d], :]
            pt = (ptf_ref, ptb_ref)[d][0, :, tok[d]]
            g_col = jnp.concatenate([jnp.broadcast_to(p[:, ci:ci + 1], (L, HEAD_DIM)),
                                     jnp.broadcast_to(p[:, ci + 1:ci + 2], (L, HEAD_DIM))], axis=1)
            i_row = jnp.concatenate([pt[ci:ci + 1, :], pt[ci + 1:ci + 2, :]], axis=1)
            b_row = jnp.concatenate([pt[ci + N_HEADS:ci + N_HEADS + 1, :],
                                     pt[ci + N_HEADS + 1:ci + N_HEADS + 2, :]], axis=1)
            m_prev = m_sc[idx]
            mask = (row <= col) if d == 0 else (row >= col)
            last = L - 1 if d == 0 else 0
            dmat = jnp.where(mask, b_row + g_col, -jnp.inf)
            a = b_row + m_prev
            m_t = jnp.maximum(a, jnp.max(dmat, axis=0, keepdims=True))
            w_intra.append(jnp.exp(dmat - m_t))
            w_inter.append(jnp.exp(a - m_t))
            floor.append(jnp.exp(-m_t))
            m_new = per_head(m_t, last)
            b_last = per_head(b_row, last)
            w_state.append(jnp.exp(b_last + m_prev - m_new))
            m_news.append(m_new)
            w_tok = jnp.exp(b_last - b_row + i_row - m_new)
            lhs = jnp.concatenate([(vts[c].astype(F32) * w_tok).astype(BF16),
                                   jnp.broadcast_to(w_tok, (pad, W)).astype(BF16)], axis=0)
            r3.append(jnp.dot(lhs, blockdiag(ks[c]), preferred_element_type=F32))

        for c, (d, pr) in enumerate(pairs):
            s = r1[c][0:L] * w_intra[c]
            den = w_inter[c] * r1[c][2 * L:2 * L + 1] + jnp.sum(s, axis=0, keepdims=True)
            inv = 1.0 / jnp.maximum(jnp.abs(den), floor[c])
            ht = (jnp.dot(vts[c], blockdiag(s.astype(BF16)), preferred_element_type=F32)
                  + r1[c][L:2 * L] * w_inter[c]) * inv
            out_ref = (hf_ref, hb_ref)[d]
            out_ref[0, (2 * pr) * HEAD_DIM:(2 * pr + 1) * HEAD_DIM, tok[d]] = ht[:, :HEAD_DIM]
            out_ref[0, (2 * pr + 1) * HEAD_DIM:(2 * pr + 2) * HEAD_DIM, tok[d]] = ht[:, HEAD_DIM:]

        for c, (d, pr) in enumerate(pairs):
            idx = d * (N_HEADS // 2) + pr
            c_sc[idx] = w_state[c] * cts[c] + r3[c][0:HEAD_DIM]
            n_sc[idx] = w_state[c] * nvs[c] + r3[c][HEAD_DIM:HEAD_DIM + 1]
            m_sc[idx] = m_news[c]

    for u in range(MLSTM_CHUNKS_PER_STEP):
        chunk_step(u, MLSTM_CHUNKS_PER_STEP - 1 - u)

    @pl.when(j == nc - 1)
    def _():
        c1_ref[0] = c_sc[...]
        n1_ref[0] = n_sc[...]
        m1_ref[0] = m_sc[...]


def _mlstm_zero_state(bsz):
    npair = N_HEADS
    return (jnp.zeros((bsz, npair, HEAD_DIM, 2 * HEAD_DIM), F32),
            jnp.zeros((bsz, npair, 1, 2 * HEAD_DIM), F32),
            jnp.zeros((bsz, npair, 1, 2 * HEAD_DIM), F32))


def _mlstm(qk, vt, p, pt, state):
    b, n, w = qk.shape
    span = CHUNK * MLSTM_CHUNKS_PER_STEP
    nc = n // span
    ng = 4 * N_HEADS
    dm = N_HEADS * HEAD_DIM
    c0, n0, m0 = state
    fwd = lambda bi, j: (bi, j, 0)
    bwd = lambda bi, j: (bi, nc - 1 - j, 0)
    fwd_t = lambda bi, j: (bi, 0, j)
    bwd_t = lambda bi, j: (bi, 0, nc - 1 - j)
    st4 = lambda bi, j: (bi, 0, 0, 0)
    outs = pl.pallas_call(
        _mlstm_kernel,
        out_shape=(jax.ShapeDtypeStruct((b, dm, n), F32),
                   jax.ShapeDtypeStruct((b, dm, n), F32),
                   jax.ShapeDtypeStruct(c0.shape, F32),
                   jax.ShapeDtypeStruct(n0.shape, F32),
                   jax.ShapeDtypeStruct(m0.shape, F32)),
        grid=(b, nc),
        in_specs=[pl.BlockSpec((1, span, w), fwd),
                  pl.BlockSpec((1, span, w), bwd),
                  pl.BlockSpec((1, dm, span), fwd_t),
                  pl.BlockSpec((1, dm, span), bwd_t),
                  pl.BlockSpec((1, span, ng), fwd),
                  pl.BlockSpec((1, span, ng), bwd),
                  pl.BlockSpec((1, ng, span), fwd_t),
                  pl.BlockSpec((1, ng, span), bwd_t),
                  pl.BlockSpec((1,) + c0.shape[1:], st4),
                  pl.BlockSpec((1,) + n0.shape[1:], st4),
                  pl.BlockSpec((1,) + m0.shape[1:], st4)],
        out_specs=(pl.BlockSpec((1, dm, span), fwd_t),
                   pl.BlockSpec((1, dm, span), bwd_t),
                   pl.BlockSpec((1,) + c0.shape[1:], st4),
                   pl.BlockSpec((1,) + n0.shape[1:], st4),
                   pl.BlockSpec((1,) + m0.shape[1:], st4)),
        scratch_shapes=[pltpu.VMEM(c0.shape[1:], F32),
                        pltpu.VMEM(n0.shape[1:], F32),
                        pltpu.VMEM(m0.shape[1:], F32)],
        compiler_params=_params(("arbitrary", "arbitrary")),
        name="mlstm",
    )(qk, qk, vt, vt, p, p, pt, pt, c0, n0, m0)
    return outs[0], outs[1], (outs[2], outs[3], outs[4])


def _mixout_kernel(convo_ref, hf_ref, hb_ref, x_ref, mod_ref, cw_ref, cb_ref, gh_ref, wo_ref, gn2_ref, wr_ref,
                   xo_ref, h2_ref, lt_ref, *, tm, row_len):
    dc = cw_ref.shape[1]
    cv = convo_ref[0]
    bg = cv[:, 0:dc].astype(F32)
    cg = cv[:, dc:2 * dc].astype(F32)
    xi = cv[:, 2 * dc:3 * dc].astype(F32)
    og = cv[:, 3 * dc:].astype(F32)
    u = cg * xi
    t = lax.broadcasted_iota(jnp.int32, (tm, dc), 0) % row_len
    u_prev = jnp.where(t == 0, 0.0, pltpu.roll(u, 1, axis=0))
    u_next = jnp.where(t == row_len - 1, 0.0, pltpu.roll(u, tm - 1, axis=0))
    cw = cw_ref[...]
    yc = bg * (cw[0:1] * u_prev + cw[1:2] * u + cw[2:3] * u_next + cb_ref[...])

    hm = hf_ref[0] + hb_ref[0]
    parts = []
    for hd in range(N_HEADS):
        hh = hm[hd * HEAD_DIM:(hd + 1) * HEAD_DIM, :]
        parts.append(hh * lax.rsqrt(jnp.mean(hh * hh, axis=0, keepdims=True) + EPS))
    ym = _sigmoid(og) * (jnp.concatenate(parts, axis=0).T * gh_ref[...])

    cat = jnp.concatenate([yc, ym], axis=-1).astype(BF16)
    y = jnp.dot(cat, wo_ref[...], preferred_element_type=F32)
    m = mod_ref[0]
    xn = x_ref[0] + m[2:3] * y
    xo_ref[0] = xn
    ms = jnp.mean(xn * xn, axis=-1, keepdims=True)
    h2 = (xn * lax.rsqrt(ms + EPS) * gn2_ref[...]) * (1.0 + m[4:5]) + m[3:4]
    hi = h2.astype(BF16)
    h2_ref[0] = hi
    lo = (h2 - hi.astype(F32)).astype(BF16)
    lg = jnp.dot(jnp.concatenate([hi, hi, lo], axis=1), wr_ref[...], preferred_element_type=F32)
    for r in range(tm // LANES):
        lt_ref[0, :, r * LANES:(r + 1) * LANES] = lg[r * LANES:(r + 1) * LANES, :].T[:N_EXPERTS, :]


def _mixout(convo, hf, hb, x, mod_l, mod_row, cw, cb, gh, wo, gn2, wr, tm, row_len):
    b, n, d = x.shape
    per_sample = mod_row is None
    mod_map = (lambda bi, i: (bi, 0, 0)) if per_sample else (lambda bi, i: (mod_row, 0, 0))
    tok = lambda bi, i: (bi, i, 0)
    cst = lambda bi, i: (0, 0)
    return pl.pallas_call(
        functools.partial(_mixout_kernel, tm=tm, row_len=row_len),
        out_shape=(jax.ShapeDtypeStruct((b, n, d), F32),
                   jax.ShapeDtypeStruct((b, n, d), BF16),
                   jax.ShapeDtypeStruct((b, N_EXPERTS, n), F32)),
        grid=(b, n // tm),
        in_specs=[pl.BlockSpec((1, tm, convo.shape[2]), tok),
                  pl.BlockSpec((1, hf.shape[1], tm), lambda bi, i: (bi, 0, i)),
                  pl.BlockSpec((1, hb.shape[1], tm), lambda bi, i: (bi, 0, i)),
                  pl.BlockSpec((1, tm, d), tok),
                  pl.BlockSpec((1, N_MOD, d), mod_map),
                  pl.BlockSpec(cw.shape, cst),
                  pl.BlockSpec(cb.shape, cst),
                  pl.BlockSpec(gh.shape, cst),
                  pl.BlockSpec(wo.shape, cst),
                  pl.BlockSpec(gn2.shape, cst),
                  pl.BlockSpec(wr.shape, cst)],
        out_specs=(pl.BlockSpec((1, tm, d), tok),
                   pl.BlockSpec((1, tm, d), tok),
                   pl.BlockSpec((1, N_EXPERTS, tm), lambda bi, i: (bi, 0, i))),
        compiler_params=_params(("arbitrary", "arbitrary")),
        name="mixout",
    )(convo, hf, hb, x, mod_l, cw, cb, gh, wo, gn2, wr)


def _route_kernel(lt_ref, pos_ref, g_ref, cs_ref, aff_sc, sel_sc, *, n, cap):
    lg = lt_ref[0]
    e = jnp.exp(lg - jnp.max(lg, axis=0, keepdims=True))
    aff_sc[...] = e / jnp.sum(e, axis=0, keepdims=True)
    capf = jnp.float32(cap)

    def count_ge(cand):
        acc = jnp.zeros((N_EXPERTS, LANES), F32)
        for j in range(n // LANES):
            acc = acc + jnp.where(aff_sc[:, j * LANES:(j + 1) * LANES] >= cand, 1.0, 0.0)
        return jnp.sum(acc, axis=1, keepdims=True)

    def exp_step(_, c):
        elo, ehi = c
        emid = jnp.floor((elo + ehi) * 0.5)
        ok = count_ge(jnp.exp2(emid)) >= capf
        return jnp.where(ok, emid, elo), jnp.where(ok, ehi, emid)

    elo, ehi = lax.fori_loop(0, 8, exp_step, (jnp.full((N_EXPERTS, 1), -128.0, F32), jnp.ones((N_EXPERTS, 1), F32)))

    def val_step(_, c):
        lo, hi = c
        mid = lo + (hi - lo) * 0.5
        ok = count_ge(mid) >= capf
        return jnp.where(ok, mid, lo), jnp.where(ok, hi, mid)

    lo, hi = lax.fori_loop(0, 40, val_step, (jnp.where(elo <= -128.0, 0.0, jnp.exp2(elo)), jnp.exp2(ehi)))
    need = capf - count_ge(hi)

    row = lax.broadcasted_iota(jnp.int32, (LANES, LANES), 0)
    col = lax.broadcasted_iota(jnp.int32, (LANES, LANES), 1)
    upper = jnp.where(row < col, 1.0, 0.0).astype(BF16)
    lane = lax.broadcasted_iota(jnp.int32, (N_EXPERTS, LANES), 1)
    per_tile = ROUTE_TILE // LANES

    ceq = jnp.zeros((N_EXPERTS, 1), F32)
    for c in range(n // LANES):
        a = aff_sc[:, c * LANES:(c + 1) * LANES]
        tie = (a >= lo) & (a < hi)
        eq = jnp.where(tie, 1.0, 0.0)
        peq = jnp.dot(eq.astype(BF16), upper, preferred_element_type=F32) + ceq
        sel = (a >= hi) | (tie & (peq < need))
        g_ref[0, :, c * LANES:(c + 1) * LANES] = jnp.where(sel, a, 0.0)
        sel_sc[:, c * LANES:(c + 1) * LANES] = jnp.where(sel, 1.0, 0.0)
        ceq = ceq + jnp.sum(eq, axis=1, keepdims=True)

    csel = jnp.zeros((N_EXPERTS, 1), F32)
    offs = jnp.zeros((N_EXPERTS, LANES), jnp.int32)
    for c in range(n // LANES):
        self = sel_sc[:, c * LANES:(c + 1) * LANES]
        psel = jnp.dot(self.astype(BF16), upper, preferred_element_type=F32) + csel
        pos_ref[0, :, c * LANES:(c + 1) * LANES] = jnp.where(self > 0.0, psel.astype(jnp.int32), -1)
        if c % per_tile == 0:
            offs = jnp.where(lane == c // per_tile, csel.astype(jnp.int32), offs)
        csel = csel + jnp.sum(self, axis=1, keepdims=True)
    cs_ref[0] = jnp.where(lane >= n // ROUTE_TILE, csel.astype(jnp.int32), offs)


def _route(lt):
    b, ne, n = lt.shape
    cap = CAPACITY_FACTOR * n // N_EXPERTS
    blk = pl.BlockSpec((1, ne, n), lambda bi: (bi, 0, 0))
    return pl.pallas_call(
        functools.partial(_route_kernel, n=n, cap=cap),
        out_shape=(jax.ShapeDtypeStruct((b, ne, n), jnp.int32),
                   jax.ShapeDtypeStruct((b, ne, n), F32),
                   jax.ShapeDtypeStruct((b, ne, LANES), jnp.int32)),
        grid=(b,),
        in_specs=[blk],
        out_specs=(blk, blk, pl.BlockSpec((1, ne, LANES), lambda bi: (bi, 0, 0))),
        scratch_shapes=[pltpu.VMEM((ne, n), F32), pltpu.VMEM((ne, n), F32)],
        compiler_params=_params(("arbitrary",)),
        name="route",
    )(lt)


def _ffn_kernel(cs_ref, h2_ref, pos_ref, wg_ref, wu_ref, wd_ref, ye_ref, xe_sc, *, tdma, nt_pad, cap, row_chunk):
    e = pl.program_id(0)
    b = pl.program_id(1)
    s = pl.program_id(2)
    ns = pl.num_programs(2)
    sub = tdma // ROUTE_TILE

    @pl.when(s == 0)
    def _():
        xe_sc[...] = jnp.zeros_like(xe_sc)

    cs_base = (b * N_EXPERTS + e) * nt_pad + s * sub

    jrow1 = lax.broadcasted_iota(jnp.int32, (FIRST_CHUNK, ROUTE_TILE), 0)
    for i in range(sub):
        base = pl.multiple_of(_floor_to(cs_ref[cs_base + i], 8), 8)
        prow = pos_ref[0, 0, :, i * ROUTE_TILE:(i + 1) * ROUTE_TILE]
        xt = h2_ref[0, i * ROUTE_TILE:(i + 1) * ROUTE_TILE, :]
        onehot = jnp.where(prow == base + jrow1, 1.0, 0.0).astype(BF16)
        xe_sc[pl.ds(base, FIRST_CHUNK), :] += jnp.dot(onehot, xt, preferred_element_type=F32)

    jrow = lax.broadcasted_iota(jnp.int32, (SLOT_CHUNK, ROUTE_TILE), 0)

    def tile_body(i, carry):
        first_end = _floor_to(cs_ref[cs_base + i], 8) + FIRST_CHUNK
        c1 = cs_ref[cs_base + i + 1]
        nch = jnp.where(c1 > first_end, _ceil_div(c1 - first_end, SLOT_CHUNK), 0)
        off = pl.multiple_of(i * ROUTE_TILE, ROUTE_TILE)

        def chunk(r, c):
            rb = pl.multiple_of(first_end + r * SLOT_CHUNK, 8)
            prow = pos_ref[0, 0, :, pl.ds(off, ROUTE_TILE)]
            onehot = jnp.where(prow == rb + jrow, 1.0, 0.0).astype(BF16)
            xe_sc[pl.ds(rb, SLOT_CHUNK), :] += jnp.dot(onehot, h2_ref[0, pl.ds(off, ROUTE_TILE), :],
                                                       preferred_element_type=F32)
            return c

        return lax.fori_loop(0, nch, chunk, carry)

    lax.fori_loop(0, sub, tile_body, 0)

    @pl.when(s == ns - 1)
    def _():
        for rc in range(cap // row_chunk):
            rows = slice(rc * row_chunk, (rc + 1) * row_chunk)
            xb = xe_sc[rows, :].astype(BF16)
            a = jnp.dot(xb, wg_ref[0], preferred_element_type=F32)
            u = jnp.dot(xb, wu_ref[0], preferred_element_type=F32)
            hm = (a * _sigmoid(a) * u).astype(BF16)
            ye_ref[0, 0, rows, :] = jnp.dot(hm, wd_ref[0], preferred_element_type=F32).astype(BF16)
        ye_ref[0, 0, cap:, :] = jnp.zeros((ye_ref.shape[2] - cap, ye_ref.shape[3]), BF16)


def _ffn(cs_flat, nt_pad, h2, pos4, wg, wu, wd):
    b, n, d = h2.shape
    ne, _, f = wg.shape
    cap = CAPACITY_FACTOR * n // N_EXPERTS
    rows = cap + SLOT_PAD
    tdma = min(n, 4096)
    row_chunk = min(cap, 512)
    grid_spec = pltpu.PrefetchScalarGridSpec(
        num_scalar_prefetch=1,
        grid=(ne, b, n // tdma),
        in_specs=[pl.BlockSpec((1, tdma, d), lambda e, bi, s, cs: (bi, s, 0)),
                  pl.BlockSpec((1, 1, 1, tdma), lambda e, bi, s, cs: (bi, e, 0, s)),
                  pl.BlockSpec((1, d, f), lambda e, bi, s, cs: (e, 0, 0)),
                  pl.BlockSpec((1, d, f), lambda e, bi, s, cs: (e, 0, 0)),
                  pl.BlockSpec((1, f, d), lambda e, bi, s, cs: (e, 0, 0))],
        out_specs=pl.BlockSpec((1, 1, rows, d), lambda e, bi, s, cs: (bi, e, 0, 0)),
        scratch_shapes=[pltpu.VMEM((rows, d), F32)])
    return pl.pallas_call(
        functools.partial(_ffn_kernel, tdma=tdma, nt_pad=nt_pad, cap=cap, row_chunk=row_chunk),
        out_shape=jax.ShapeDtypeStruct((b, ne, rows, d), BF16),
        grid_spec=grid_spec,
        compiler_params=_params(("arbitrary", "arbitrary", "arbitrary")),
        name="ffn",
    )(cs_flat, h2, pos4, wg, wu, wd)


def _combine_kernel(cs_ref, pos_ref, g_ref, x_ref, mod_ref, gfin_ref, ye_hbm, out_ref,
                    ystage, pt_sc, acc_sc, sem, *, nt_pad, final):
    b = pl.program_id(0)
    i = pl.program_id(1)
    nt = pl.num_programs(1)
    step = b * nt + i
    slot = step % 2
    group = 4 * SLOT_CHUNK

    @pl.when(step == 0)
    def _():
        ystage[...] = jnp.zeros_like(ystage)

    def chunk_plan(bb, ii, e):
        c0 = cs_ref[(bb * N_EXPERTS + e) * nt_pad + ii]
        c1 = cs_ref[(bb * N_EXPERTS + e) * nt_pad + ii + 1]
        base = _floor_to(c0, 16)
        return base, jnp.where(c1 > c0, _ceil_div(c1 - base, SLOT_CHUNK), 0)

    def chunk_copy(bb, e, rb, sl, k):
        return pltpu.make_async_copy(ye_hbm.at[bb, e, pl.ds(rb, SLOT_CHUNK), :],
                                     ystage.at[sl, pl.ds(k * SLOT_CHUNK, SLOT_CHUNK), :], sem.at[sl])

    def issue(bb, ii, sl):
        k = jnp.int32(0)
        for e in range(N_EXPERTS):
            base, nch = chunk_plan(bb, ii, e)

            def start(r, kk, e=e, base=base):
                chunk_copy(bb, e, pl.multiple_of(base + r * SLOT_CHUNK, 16), sl, kk).start()
                return kk + 1

            k = lax.fori_loop(0, nch, start, k)

    @pl.when(step == 0)
    def _():
        issue(b, i, slot)

    @pl.when(step + 1 < pl.num_programs(0) * nt)
    def _():
        wrap = i + 1 == nt
        issue(jnp.where(wrap, b + 1, b), jnp.where(wrap, 0, i + 1), 1 - slot)

    jrow = lax.broadcasted_iota(jnp.int32, (SLOT_CHUNK, ROUTE_TILE), 0)
    k = jnp.int32(0)
    for e in range(N_EXPERTS):
        base, nch = chunk_plan(b, i, e)
        prow = pos_ref[0, e:e + 1, :]
        grow = g_ref[0, e:e + 1, :]

        def weights(r, kk, base=base, prow=prow, grow=grow):
            row0 = pl.multiple_of(kk * SLOT_CHUNK, SLOT_CHUNK)
            pt_sc[pl.ds(row0, SLOT_CHUNK), :] = jnp.where(prow == base + r * SLOT_CHUNK + jrow, grow, 0.0)
            return kk + 1

        k = lax.fori_loop(0, nch, weights, k)

    kpad = jnp.maximum(_ceil_div(k, 4) * 4, COMBINE_STATIC_CHUNKS)

    def zero_pad(kk, c):
        row0 = pl.multiple_of(kk * SLOT_CHUNK, SLOT_CHUNK)
        pt_sc[pl.ds(row0, SLOT_CHUNK), :] = jnp.zeros((SLOT_CHUNK, ROUTE_TILE), F32)
        return c

    lax.fori_loop(k, kpad, zero_pad, 0)

    def wait_one(kk, c):
        chunk_copy(0, 0, 0, slot, kk).wait()
        return c

    lax.fori_loop(0, k, wait_one, 0)

    static_rows = COMBINE_STATIC_CHUNKS * SLOT_CHUNK
    w = pt_sc[0:static_rows, :].T.astype(BF16)
    acc_sc[...] = jnp.dot(w, ystage[slot, 0:static_rows, :], preferred_element_type=F32)

    def matmul_group(gi, c):
        row0 = pl.multiple_of(gi * group, group)
        wg = pt_sc[pl.ds(row0, group), :].T.astype(BF16)
        acc_sc[...] += jnp.dot(wg, ystage[slot, pl.ds(row0, group), :], preferred_element_type=F32)
        return c

    lax.fori_loop(COMBINE_STATIC_CHUNKS // 4, kpad // 4, matmul_group, 0)

    xn = x_ref[0] + mod_ref[0][5:6] * acc_sc[...]
    if final:
        ms = jnp.mean(xn * xn, axis=-1, keepdims=True)
        xn = xn * lax.rsqrt(ms + EPS) * gfin_ref[...]
    out_ref[0] = xn


def _combine(cs_flat, nt_pad, pos, g, x, mod_l, mod_row, gfin, ye, final):
    b, n, d = x.shape
    per_sample = mod_row is None
    mod_map = ((lambda bi, i, cs: (bi, 0, 0)) if per_sample else (lambda bi, i, cs: (mod_row, 0, 0)))
    max_chunks = N_EXPERTS * ((ROUTE_TILE + 15 + SLOT_CHUNK - 1) // SLOT_CHUNK) + 4
    grid_spec = pltpu.PrefetchScalarGridSpec(
        num_scalar_prefetch=1,
        grid=(b, n // ROUTE_TILE),
        in_specs=[pl.BlockSpec((1, N_EXPERTS, ROUTE_TILE), lambda bi, i, cs: (bi, 0, i)),
                  pl.BlockSpec((1, N_EXPERTS, ROUTE_TILE), lambda bi, i, cs: (bi, 0, i)),
                  pl.BlockSpec((1, ROUTE_TILE, d), lambda bi, i, cs: (bi, i, 0)),
                  pl.BlockSpec((1, N_MOD, d), mod_map),
                  pl.BlockSpec((1, d), lambda bi, i, cs: (0, 0)),
                  pl.BlockSpec(memory_space=pl.ANY)],
        out_specs=pl.BlockSpec((1, ROUTE_TILE, d), lambda bi, i, cs: (bi, i, 0)),
        scratch_shapes=[pltpu.VMEM((2, max_chunks * SLOT_CHUNK, d), BF16),
                        pltpu.VMEM((max_chunks * SLOT_CHUNK, ROUTE_TILE), F32),
                        pltpu.VMEM((ROUTE_TILE, d), F32),
                        pltpu.SemaphoreType.DMA((2,))])
    return pl.pallas_call(
        functools.partial(_combine_kernel, nt_pad=nt_pad, final=final),
        out_shape=jax.ShapeDtypeStruct((b, n, d), F32),
        grid_spec=grid_spec,
        compiler_params=_params(("arbitrary", "arbitrary")),
        name="combine",
    )(cs_flat, pos, g, x, mod_l, gfin, ye)


def _moe(lt, h2, x, mod_l, mod_row, gfin, wg, wu, wd, final):
    b, n, _ = x.shape
    pos, g, cs = _route(lt)
    nt_pad = ((n // ROUTE_TILE + 1 + 7) // 8) * 8
    cs_flat = cs[:, :, :nt_pad].reshape(-1)
    ye = _ffn(cs_flat, nt_pad, h2, pos.reshape(b, N_EXPERTS, 1, n), wg, wu, wd)
    return _combine(cs_flat, nt_pad, pos, g, x, mod_l, mod_row, gfin, ye, final)


def kernel(x, c, ctx, c_ctx, w_mod, b_mod, g_norm1, g_norm2, w_in, w_out, conv_w, conv_b, gate_b, g_head,
           w_router, w_gate_e, w_up_e, w_down_e, g_final):
    bsz, n, d = x.shape
    nctx = ctx.shape[1]
    depth = w_mod.shape[0]
    dm = N_HEADS * HEAD_DIM
    dc = conv_w.shape[2]
    ng = 4 * N_HEADS
    ctx_row = bsz

    rows = ((bsz + 1 + 7) // 8) * 8
    cc = jnp.zeros((rows, d), F32).at[:bsz].set(c).at[bsz].set(c_ctx)
    mod = _modulation(cc, w_mod, b_mod).reshape(depth, rows, N_MOD, d)
    gfin = g_final.reshape(1, d)

    xl, xc = x, ctx
    for layer in range(depth):
        last = layer == depth - 1
        mod_l = mod[layer]
        wi = w_in[layer]
        wa = jnp.concatenate([wi[:, :3 * dc], wi[:, 3 * dc + 3 * dm:3 * dc + 4 * dm]], axis=1).astype(BF16)
        wb = wi[:, 3 * dc:3 * dc + 3 * dm].astype(BF16)
        wgt = jnp.zeros((d, LANES), BF16).at[:, :ng].set(wi[:, 3 * dc + 4 * dm:].astype(BF16))
        gb = jnp.zeros((1, LANES), F32).at[0, :ng].set(gate_b[layer].reshape(-1))
        gn1 = g_norm1[layer].reshape(1, d)
        gn2 = g_norm2[layer].reshape(1, d)
        cw = jnp.zeros((8, dc), F32).at[:conv_w.shape[1]].set(conv_w[layer])
        cb = conv_b[layer].reshape(1, dc)
        gh = g_head[layer].reshape(1, dm)
        wo = w_out[layer].astype(BF16)
        wr32 = jnp.zeros((d, LANES), F32).at[:, :N_EXPERTS].set(w_router[layer])
        wr_hi = wr32.astype(BF16)
        wr = jnp.concatenate([wr_hi, (wr32 - wr_hi.astype(F32)).astype(BF16), wr_hi], axis=0)
        wge = w_gate_e[layer].astype(BF16)
        wue = w_up_e[layer].astype(BF16)
        wde = w_down_e[layer].astype(BF16)

        convo_c, qk_c, vt_c, p_c, pt_c = _inproj(xc, mod_l, ctx_row, gn1, wa, wb, wgt, gb, tm=nctx)
        convo_l, qk_l, vt_l, p_l, pt_l = _inproj(xl, mod_l, None, gn1, wa, wb, wgt, gb, tm=512)
        zero_state = _mlstm_zero_state(bsz)
        hf_c, hb_c, state = _mlstm(qk_c, vt_c, p_c, pt_c, zero_state)
        hf_l, hb_l, _ = _mlstm(qk_l, vt_l, p_l, pt_l, state)

        xl, h2_l, lt_l = _mixout(convo_l, hf_l, hb_l, xl, mod_l, None, cw, cb, gh, wo, gn2, wr,
                                 tm=512, row_len=GRID_W)
        xl = _moe(lt_l, h2_l, xl, mod_l, None, gfin, wge, wue, wde, final=last)
        if not last:
            xc, h2_c, lt_c = _mixout(convo_c, hf_c, hb_c, xc, mod_l, ctx_row, cw, cb, gh, wo, gn2, wr,
                                     tm=nctx, row_len=nctx)
            xc = _moe(lt_c, h2_c, xc, mod_l, ctx_row, gfin, wge, wue, wde, final=False)
    return xl
```

```python
import functools

import jax
import jax.numpy as jnp
from jax import lax
from jax.experimental import pallas as pl
from jax.experimental.pallas import tpu as pltpu

EPS = 1e-6
N_HEADS = 4
HEAD_DIM = 128
CHUNK = 128
MLSTM_CHUNKS_PER_STEP = 2
N_EXPERTS = 16
CAPACITY_FACTOR = 2
N_MOD = 6
GRID_W = 64
LANES = 128
ROUTE_TILE = 256
SLOT_CHUNK = 64
FIRST_CHUNK = 64
FFN_ROWS = 512
SLOT_PAD = 128
COMBINE_STATIC_CHUNKS = 20
VMEM_LIMIT = 56 * 1024 * 1024

F32 = jnp.float32
BF16 = jnp.bfloat16
HIGHEST = lax.Precision.HIGHEST


def _params(sem, vmem=VMEM_LIMIT):
    return pltpu.CompilerParams(dimension_semantics=sem, vmem_limit_bytes=vmem)


def _sigmoid(x):
    return 1.0 / (1.0 + jnp.exp(-x))


def _floor_to(x, m):
    return jnp.bitwise_and(x, -m)


def _ceil_div(x, m):
    return jnp.right_shift(x + (m - 1), m.bit_length() - 1)


def _mod_kernel(c_ref, w_ref, b_ref, o_ref):
    c = c_ref[...]
    s = c * _sigmoid(c)
    o_ref[0] = jnp.dot(s, w_ref[0], precision=HIGHEST, preferred_element_type=F32) + b_ref[0]


def _modulation(cc, w_mod, b_mod):
    depth, d, nm = w_mod.shape
    rows = cc.shape[0]
    tn = 1536
    return pl.pallas_call(
        _mod_kernel,
        out_shape=jax.ShapeDtypeStruct((depth, rows, nm), F32),
        grid=(depth, nm // tn),
        in_specs=[pl.BlockSpec((rows, d), lambda l, j: (0, 0)),
                  pl.BlockSpec((1, d, tn), lambda l, j: (l, 0, j)),
                  pl.BlockSpec((1, 1, tn), lambda l, j: (l, 0, j))],
        out_specs=pl.BlockSpec((1, rows, tn), lambda l, j: (l, 0, j)),
        compiler_params=_params(("arbitrary", "arbitrary")),
        name="modulation",
    )(cc, w_mod, b_mod.reshape(depth, 1, nm))


def _inproj_kernel(x_ref, mod_ref, gn_ref, wa_ref, wb_ref, wg_ref, gb_ref,
                   convo_ref, qk_ref, vt_ref, p_ref, pt_ref, *, tm):
    x = x_ref[0]
    m = mod_ref[0]
    ms = jnp.mean(x * x, axis=-1, keepdims=True)
    h = (x * lax.rsqrt(ms + EPS) * gn_ref[...]) * (1.0 + m[1:2]) + m[0:1]
    hb = h.astype(BF16)
    cw = 512
    for c in range(wa_ref.shape[1] // cw):
        convo_ref[0, :, c * cw:(c + 1) * cw] = jnp.dot(
            hb, wa_ref[:, c * cw:(c + 1) * cw], preferred_element_type=F32).astype(BF16)
    q = jnp.dot(hb, wb_ref[:, 0:cw], preferred_element_type=F32) * (HEAD_DIM ** -0.5)
    qk_ref[0, :, 0:cw] = q.astype(BF16)
    qk_ref[0, :, cw:2 * cw] = jnp.dot(hb, wb_ref[:, cw:2 * cw], preferred_element_type=F32).astype(BF16)
    v = jnp.dot(hb, wb_ref[:, 2 * cw:3 * cw], preferred_element_type=F32)
    vt_ref[0] = v.T.astype(BF16)

    g = jnp.dot(hb, wg_ref[...], preferred_element_type=F32) + gb_ref[...]
    kind = lax.broadcasted_iota(jnp.int32, (1, LANES), 1) // N_HEADS
    logf = jnp.minimum(g, 0.0) - jnp.log1p(jnp.exp(-jnp.abs(g)))
    g = jnp.where((kind == 1) | (kind == 3), logf, g)
    row = lax.broadcasted_iota(jnp.int32, (CHUNK, CHUNK), 0)
    col = lax.broadcasted_iota(jnp.int32, (CHUNK, CHUNK), 1)
    tri = jnp.concatenate([jnp.where(col <= row, 1.0, 0.0), jnp.where(col >= row, 1.0, 0.0)], axis=0).astype(BF16)
    for r in range(tm // CHUNK):
        gc = g[r * CHUNK:(r + 1) * CHUNK]
        g_hi = gc.astype(BF16)
        rem = gc - g_hi.astype(F32)
        g_mid = rem.astype(BF16)
        g_lo = (rem - g_mid.astype(F32)).astype(BF16)
        sums = jnp.dot(tri, jnp.concatenate([g_hi, g_mid, g_lo], axis=1), preferred_element_type=F32)
        sums = sums[:, 0:LANES] + sums[:, LANES:2 * LANES] + sums[:, 2 * LANES:3 * LANES]
        cum, suf = sums[0:CHUNK], sums[CHUNK:2 * CHUNK]
        p = jnp.where(kind == 1, cum, jnp.where(kind == 3, suf, gc))
        pt_ref[0, :, r * CHUNK:(r + 1) * CHUNK] = p.T[:4 * N_HEADS, :]
        diff = p - pltpu.roll(p, LANES - N_HEADS, axis=1)
        p_ref[0, r * CHUNK:(r + 1) * CHUNK, :] = diff[:, :4 * N_HEADS]


def _inproj(x, mod_l, mod_row, gn, wa, wb, wg, gb, tm):
    b, n, d = x.shape
    per_sample = mod_row is None
    mod_map = (lambda bi, i: (bi, 0, 0)) if per_sample else (lambda bi, i: (mod_row, 0, 0))
    ng = 4 * N_HEADS
    dm = N_HEADS * HEAD_DIM
    return pl.pallas_call(
        functools.partial(_inproj_kernel, tm=tm),
        out_shape=(jax.ShapeDtypeStruct((b, n, wa.shape[1]), BF16),
                   jax.ShapeDtypeStruct((b, n, 2 * dm), BF16),
                   jax.ShapeDtypeStruct((b, dm, n), BF16),
                   jax.ShapeDtypeStruct((b, n, ng), F32),
                   jax.ShapeDtypeStruct((b, ng, n), F32)),
        grid=(b, n // tm),
        in_specs=[pl.BlockSpec((1, tm, d), lambda bi, i: (bi, i, 0)),
                  pl.BlockSpec((1, N_MOD, d), mod_map),
                  pl.BlockSpec((1, d), lambda bi, i: (0, 0)),
                  pl.BlockSpec(wa.shape, lambda bi, i: (0, 0)),
                  pl.BlockSpec(wb.shape, lambda bi, i: (0, 0)),
                  pl.BlockSpec(wg.shape, lambda bi, i: (0, 0)),
                  pl.BlockSpec((1, LANES), lambda bi, i: (0, 0))],
        out_specs=(pl.BlockSpec((1, tm, wa.shape[1]), lambda bi, i: (bi, i, 0)),
                   pl.BlockSpec((1, tm, 2 * dm), lambda bi, i: (bi, i, 0)),
                   pl.BlockSpec((1, dm, tm), lambda bi, i: (bi, 0, i)),
                   pl.BlockSpec((1, tm, ng), lambda bi, i: (bi, i, 0)),
                   pl.BlockSpec((1, ng, tm), lambda bi, i: (bi, 0, i))),
        compiler_params=_params(("arbitrary", "arbitrary")),
        name="inproj",
    )(x, mod_l, gn, wa, wb, wg, gb)


def _mlstm_kernel(qf_ref, qb_ref, vf_ref, vb_ref, pf_ref, pb_ref, ptf_ref, ptb_ref, c0_ref, n0_ref, m0_ref,
                  hf_ref, hb_ref, c1_ref, n1_ref, m1_ref, c_sc, n_sc, m_sc):
    j = pl.program_id(1)
    nc = pl.num_programs(1)
    L = CHUNK
    dm = N_HEADS * HEAD_DIM
    pad = 16

    @pl.when(j == 0)
    def _():
        c_sc[...] = c0_ref[0]
        n_sc[...] = n0_ref[0]
        m_sc[...] = m0_ref[0]

    W = 2 * HEAD_DIM
    row = lax.broadcasted_iota(jnp.int32, (L, W), 0)
    col = lax.broadcasted_iota(jnp.int32, (L, W), 1) % L
    nt_dims = (((1,), (1,)), ((), ()))
    zeros = jnp.zeros((L, HEAD_DIM), BF16)
    pairs = [(d, pr) for d in range(2) for pr in range(N_HEADS // 2)]

    def blockdiag(x):
        return jnp.concatenate([jnp.concatenate([x[:, :HEAD_DIM], zeros], axis=1),
                                jnp.concatenate([zeros, x[:, HEAD_DIM:]], axis=1)], axis=0)

    def per_head(x, lane):
        return jnp.concatenate([jnp.broadcast_to(x[:, lane:lane + 1], (1, HEAD_DIM)),
                                jnp.broadcast_to(x[:, HEAD_DIM + lane:HEAD_DIM + lane + 1], (1, HEAD_DIM))], axis=1)

    def chunk_step(sub_f, sub_b):
        tok = (slice(sub_f * L, (sub_f + 1) * L), slice(sub_b * L, (sub_b + 1) * L))
        ks, vts, cts, nvs, r1 = [], [], [], [], []
        for d, pr in pairs:
            idx = d * (N_HEADS // 2) + pr
            qk_ref = (qf_ref, qb_ref)[d]
            q = qk_ref[0, tok[d], pr * W:(pr + 1) * W]
            k = qk_ref[0, tok[d], dm + pr * W:dm + (pr + 1) * W]
            ct, nvec = c_sc[idx], n_sc[idx]
            lhs = jnp.concatenate([k, ct.astype(BF16), jnp.broadcast_to(nvec, (pad, W)).astype(BF16)], axis=0)
            r1.append(lax.dot_general(lhs, blockdiag(q), nt_dims, preferred_element_type=F32))
            v_ref = (vf_ref, vb_ref)[d]
            vts.append(jnp.concatenate([v_ref[0, (2 * pr) * HEAD_DIM:(2 * pr + 1) * HEAD_DIM, tok[d]],
                                        v_ref[0, (2 * pr + 1) * HEAD_DIM:(2 * pr + 2) * HEAD_DIM, tok[d]]], axis=1))
            ks.append(k), cts.append(ct), nvs.append(nvec)

        w_intra, w_inter, floor, w_state, m_news, r3 = [], [], [], [], [], []
        for c, (d, pr) in enumerate(pairs):
            idx = d * (N_HEADS // 2) + pr
            ci = 2 * d * N_HEADS + 2 * pr
            p = (pf_ref, pb_ref)[d][0, tok[d], :]
            pt = (ptf_ref, ptb_ref)[d][0, :, tok[d]]
            g_col = jnp.concatenate([jnp.broadcast_to(p[:, ci:ci + 1], (L, HEAD_DIM)),
                                     jnp.broadcast_to(p[:, ci + 1:ci + 2], (L, HEAD_DIM))], axis=1)
            i_row = jnp.concatenate([pt[ci:ci + 1, :], pt[ci + 1:ci + 2, :]], axis=1)
            b_row = jnp.concatenate([pt[ci + N_HEADS:ci + N_HEADS + 1, :],
                                     pt[ci + N_HEADS + 1:ci + N_HEADS + 2, :]], axis=1)
            m_prev = m_sc[idx]
            mask = (row <= col) if d == 0 else (row >= col)
            last = L - 1 if d == 0 else 0
            dmat = jnp.where(mask, b_row + g_col, -jnp.inf)
            a = b_row + m_prev
            m_t = jnp.maximum(a, jnp.max(dmat, axis=0, keepdims=True))
            w_intra.append(jnp.exp(dmat - m_t))
            w_inter.append(jnp.exp(a - m_t))
            floor.append(jnp.exp(-m_t))
            m_new = per_head(m_t, last)
            b_last = per_head(b_row, last)
            w_state.append(jnp.exp(b_last + m_prev - m_new))
            m_news.append(m_new)
            w_tok = jnp.exp(b_last - b_row + i_row - m_new)
            lhs = jnp.concatenate([(vts[c].astype(F32) * w_tok).astype(BF16),
                                   jnp.broadcast_to(w_tok, (pad, W)).astype(BF16)], axis=0)
            r3.append(jnp.dot(lhs, blockdiag(ks[c]), preferred_element_type=F32))

        for c, (d, pr) in enumerate(pairs):
            s = r1[c][0:L] * w_intra[c]
            den = w_inter[c] * r1[c][2 * L:2 * L + 1] + jnp.sum(s, axis=0, keepdims=True)
            inv = 1.0 / jnp.maximum(jnp.abs(den), floor[c])
            ht = (jnp.dot(vts[c], blockdiag(s.astype(BF16)), preferred_element_type=F32)
                  + r1[c][L:2 * L] * w_inter[c]) * inv
            out_ref = (hf_ref, hb_ref)[d]
            out_ref[0, (2 * pr) * HEAD_DIM:(2 * pr + 1) * HEAD_DIM, tok[d]] = ht[:, :HEAD_DIM]
            out_ref[0, (2 * pr + 1) * HEAD_DIM:(2 * pr + 2) * HEAD_DIM, tok[d]] = ht[:, HEAD_DIM:]

        for c, (d, pr) in enumerate(pairs):
            idx = d * (N_HEADS // 2) + pr
            c_sc[idx] = w_state[c] * cts[c] + r3[c][0:HEAD_DIM]
            n_sc[idx] = w_state[c] * nvs[c] + r3[c][HEAD_DIM:HEAD_DIM + 1]
            m_sc[idx] = m_news[c]

    for u in range(MLSTM_CHUNKS_PER_STEP):
        chunk_step(u, MLSTM_CHUNKS_PER_STEP - 1 - u)

    @pl.when(j == nc - 1)
    def _():
        c1_ref[0] = c_sc[...]
        n1_ref[0] = n_sc[...]
        m1_ref[0] = m_sc[...]


def _mlstm_zero_state(bsz):
    npair = N_HEADS
    return (jnp.zeros((bsz, npair, HEAD_DIM, 2 * HEAD_DIM), F32),
            jnp.zeros((bsz, npair, 1, 2 * HEAD_DIM), F32),
            jnp.zeros((bsz, npair, 1, 2 * HEAD_DIM), F32))


def _mlstm(qk, vt, p, pt, state):
    b, n, w = qk.shape
    span = CHUNK * MLSTM_CHUNKS_PER_STEP
    nc = n // span
    ng = 4 * N_HEADS
    dm = N_HEADS * HEAD_DIM
    c0, n0, m0 = state
    fwd = lambda bi, j: (bi, j, 0)
    bwd = lambda bi, j: (bi, nc - 1 - j, 0)
    fwd_t = lambda bi, j: (bi, 0, j)
    bwd_t = lambda bi, j: (bi, 0, nc - 1 - j)
    st4 = lambda bi, j: (bi, 0, 0, 0)
    outs = pl.pallas_call(
        _mlstm_kernel,
        out_shape=(jax.ShapeDtypeStruct((b, dm, n), F32),
                   jax.ShapeDtypeStruct((b, dm, n), F32),
                   jax.ShapeDtypeStruct(c0.shape, F32),
                   jax.ShapeDtypeStruct(n0.shape, F32),
                   jax.ShapeDtypeStruct(m0.shape, F32)),
        grid=(b, nc),
        in_specs=[pl.BlockSpec((1, span, w), fwd),
                  pl.BlockSpec((1, span, w), bwd),
                  pl.BlockSpec((1, dm, span), fwd_t),
                  pl.BlockSpec((1, dm, span), bwd_t),
                  pl.BlockSpec((1, span, ng), fwd),
                  pl.BlockSpec((1, span, ng), bwd),
                  pl.BlockSpec((1, ng, span), fwd_t),
                  pl.BlockSpec((1, ng, span), bwd_t),
                  pl.BlockSpec((1,) + c0.shape[1:], st4),
                  pl.BlockSpec((1,) + n0.shape[1:], st4),
                  pl.BlockSpec((1,) + m0.shape[1:], st4)],
        out_specs=(pl.BlockSpec((1, dm, span), fwd_t),
                   pl.BlockSpec((1, dm, span), bwd_t),
                   pl.BlockSpec((1,) + c0.shape[1:], st4),
                   pl.BlockSpec((1,) + n0.shape[1:], st4),
                   pl.BlockSpec((1,) + m0.shape[1:], st4)),
        scratch_shapes=[pltpu.VMEM(c0.shape[1:], F32),
                        pltpu.VMEM(n0.shape[1:], F32),
                        pltpu.VMEM(m0.shape[1:], F32)],
        compiler_params=_params(("arbitrary", "arbitrary")),
        name="mlstm",
    )(qk, qk, vt, vt, p, p, pt, pt, c0, n0, m0)
    return outs[0], outs[1], (outs[2], outs[3], outs[4])


def _mixout_kernel(convo_ref, hf_ref, hb_ref, x_ref, mod_ref, cw_ref, cb_ref, gh_ref, wo_ref, gn2_ref, wr_ref,
                   xo_ref, h2_ref, lt_ref, *, tm, row_len):
    dc = cw_ref.shape[1]
    cw = cw_ref[...]
    m = mod_ref[0]
    sub = LANES if LANES % row_len == 0 else tm
    t = lax.broadcasted_iota(jnp.int32, (sub, dc), 0) % row_len
    for r in range(tm // sub):
        rows = slice(r * sub, (r + 1) * sub)
        cv = convo_ref[0, rows, :]
        bg = cv[:, 0:dc].astype(F32)
        cg = cv[:, dc:2 * dc].astype(F32)
        xi = cv[:, 2 * dc:3 * dc].astype(F32)
        og = cv[:, 3 * dc:].astype(F32)
        u = cg * xi
        u_prev = jnp.where(t == 0, 0.0, pltpu.roll(u, 1, axis=0))
        u_next = jnp.where(t == row_len - 1, 0.0, pltpu.roll(u, sub - 1, axis=0))
        yc = bg * (cw[0:1] * u_prev + cw[1:2] * u + cw[2:3] * u_next + cb_ref[...])

        hm = hf_ref[0, :, rows] + hb_ref[0, :, rows]
        parts = []
        for hd in range(N_HEADS):
            hh = hm[hd * HEAD_DIM:(hd + 1) * HEAD_DIM, :]
            parts.append(hh * lax.rsqrt(jnp.mean(hh * hh, axis=0, keepdims=True) + EPS))
        ym = _sigmoid(og) * (jnp.concatenate(parts, axis=0).T * gh_ref[...])

        cat = jnp.concatenate([yc, ym], axis=-1).astype(BF16)
        y = jnp.dot(cat, wo_ref[...], preferred_element_type=F32)
        xn = x_ref[0, rows, :] + m[2:3] * y
        xo_ref[0, rows, :] = xn
        ms = jnp.mean(xn * xn, axis=-1, keepdims=True)
        h2 = (xn * lax.rsqrt(ms + EPS) * gn2_ref[...]) * (1.0 + m[4:5]) + m[3:4]
        hi = h2.astype(BF16)
        h2_ref[0, rows, :] = hi
        lo = (h2 - hi.astype(F32)).astype(BF16)
        lg = jnp.dot(jnp.concatenate([hi, hi, lo], axis=1), wr_ref[...], preferred_element_type=F32)
        for q in range(sub // LANES):
            lanes = slice(r * sub + q * LANES, r * sub + (q + 1) * LANES)
            lt_ref[0, :, lanes] = lg[q * LANES:(q + 1) * LANES, :].T[:N_EXPERTS, :]


def _mixout(convo, hf, hb, x, mod_l, mod_row, cw, cb, gh, wo, gn2, wr, tm, row_len):
    b, n, d = x.shape
    per_sample = mod_row is None
    mod_map = (lambda bi, i: (bi, 0, 0)) if per_sample else (lambda bi, i: (mod_row, 0, 0))
    tok = lambda bi, i: (bi, i, 0)
    cst = lambda bi, i: (0, 0)
    return pl.pallas_call(
        functools.partial(_mixout_kernel, tm=tm, row_len=row_len),
        out_shape=(jax.ShapeDtypeStruct((b, n, d), F32),
                   jax.ShapeDtypeStruct((b, n, d), BF16),
                   jax.ShapeDtypeStruct((b, N_EXPERTS, n), F32)),
        grid=(b, n // tm),
        in_specs=[pl.BlockSpec((1, tm, convo.shape[2]), tok),
                  pl.BlockSpec((1, hf.shape[1], tm), lambda bi, i: (bi, 0, i)),
                  pl.BlockSpec((1, hb.shape[1], tm), lambda bi, i: (bi, 0, i)),
                  pl.BlockSpec((1, tm, d), tok),
                  pl.BlockSpec((1, N_MOD, d), mod_map),
                  pl.BlockSpec(cw.shape, cst),
                  pl.BlockSpec(cb.shape, cst),
                  pl.BlockSpec(gh.shape, cst),
                  pl.BlockSpec(wo.shape, cst),
                  pl.BlockSpec(gn2.shape, cst),
                  pl.BlockSpec(wr.shape, cst)],
        out_specs=(pl.BlockSpec((1, tm, d), tok),
                   pl.BlockSpec((1, tm, d), tok),
                   pl.BlockSpec((1, N_EXPERTS, tm), lambda bi, i: (bi, 0, i))),
        compiler_params=_params(("arbitrary", "arbitrary")),
        name="mixout",
    )(convo, hf, hb, x, mod_l, cw, cb, gh, wo, gn2, wr)


def _route_kernel(lt_ref, pos_ref, g_ref, cs_ref, aff_sc, sel_sc, *, n, cap):
    lg = lt_ref[0]
    e = jnp.exp(lg - jnp.max(lg, axis=0, keepdims=True))
    aff_sc[...] = e / jnp.sum(e, axis=0, keepdims=True)
    capf = jnp.float32(cap)

    def count_ge(cand):
        acc = jnp.zeros((N_EXPERTS, LANES), F32)
        for j in range(n // LANES):
            acc = acc + jnp.where(aff_sc[:, j * LANES:(j + 1) * LANES] >= cand, 1.0, 0.0)
        return jnp.sum(acc, axis=1, keepdims=True)

    def exp_step(_, c):
        elo, ehi = c
        emid = jnp.floor((elo + ehi) * 0.5)
        ok = count_ge(jnp.exp2(emid)) >= capf
        return jnp.where(ok, emid, elo), jnp.where(ok, ehi, emid)

    elo, ehi = lax.fori_loop(0, 8, exp_step, (jnp.full((N_EXPERTS, 1), -128.0, F32), jnp.ones((N_EXPERTS, 1), F32)))

    def val_step(_, c):
        lo, hi = c
        mid = lo + (hi - lo) * 0.5
        ok = count_ge(mid) >= capf
        return jnp.where(ok, mid, lo), jnp.where(ok, hi, mid)

    lo, hi = lax.fori_loop(0, 40, val_step, (jnp.where(elo <= -128.0, 0.0, jnp.exp2(elo)), jnp.exp2(ehi)))
    need = capf - count_ge(hi)

    row = lax.broadcasted_iota(jnp.int32, (LANES, LANES), 0)
    col = lax.broadcasted_iota(jnp.int32, (LANES, LANES), 1)
    upper = jnp.where(row < col, 1.0, 0.0).astype(BF16)
    lane = lax.broadcasted_iota(jnp.int32, (N_EXPERTS, LANES), 1)
    per_tile = ROUTE_TILE // LANES

    ceq = jnp.zeros((N_EXPERTS, 1), F32)
    for c in range(n // LANES):
        a = aff_sc[:, c * LANES:(c + 1) * LANES]
        tie = (a >= lo) & (a < hi)
        eq = jnp.where(tie, 1.0, 0.0)
        peq = jnp.dot(eq.astype(BF16), upper, preferred_element_type=F32) + ceq
        sel = (a >= hi) | (tie & (peq < need))
        g_ref[0, :, c * LANES:(c + 1) * LANES] = jnp.where(sel, a, 0.0)
        sel_sc[:, c * LANES:(c + 1) * LANES] = jnp.where(sel, 1.0, 0.0)
        ceq = ceq + jnp.sum(eq, axis=1, keepdims=True)

    csel = jnp.zeros((N_EXPERTS, 1), F32)
    offs = jnp.zeros((N_EXPERTS, LANES), jnp.int32)
    for c in range(n // LANES):
        self = sel_sc[:, c * LANES:(c + 1) * LANES]
        psel = jnp.dot(self.astype(BF16), upper, preferred_element_type=F32) + csel
        pos_ref[0, :, c * LANES:(c + 1) * LANES] = jnp.where(self > 0.0, psel.astype(jnp.int32), -1)
        if c % per_tile == 0:
            offs = jnp.where(lane == c // per_tile, csel.astype(jnp.int32), offs)
        csel = csel + jnp.sum(self, axis=1, keepdims=True)
    cs_ref[0] = jnp.where(lane >= n // ROUTE_TILE, csel.astype(jnp.int32), offs)


def _route(lt):
    b, ne, n = lt.shape
    cap = CAPACITY_FACTOR * n // N_EXPERTS
    blk = pl.BlockSpec((1, ne, n), lambda bi: (bi, 0, 0))
    return pl.pallas_call(
        functools.partial(_route_kernel, n=n, cap=cap),
        out_shape=(jax.ShapeDtypeStruct((b, ne, n), jnp.int32),
                   jax.ShapeDtypeStruct((b, ne, n), F32),
                   jax.ShapeDtypeStruct((b, ne, LANES), jnp.int32)),
        grid=(b,),
        in_specs=[blk],
        out_specs=(blk, blk, pl.BlockSpec((1, ne, LANES), lambda bi: (bi, 0, 0))),
        scratch_shapes=[pltpu.VMEM((ne, n), F32), pltpu.VMEM((ne, n), F32)],
        compiler_params=_params(("arbitrary",)),
        name="route",
    )(lt)


def _ffn_kernel(cs_ref, h2_ref, pos_ref, wg_ref, wu_ref, wd_ref, ye_ref, xe_sc, *, tdma, nt_pad, cap, row_chunk,
                group):
    e = pl.program_id(0)
    b = pl.program_id(1)
    s = pl.program_id(2)
    ns = pl.num_programs(2)
    sub = tdma // ROUTE_TILE
    g = b % group

    @pl.when(s == 0)
    def _():
        xe_sc[g] = jnp.zeros(xe_sc.shape[1:], F32)

    cs_base = (b * N_EXPERTS + e) * nt_pad + s * sub

    jrow1 = lax.broadcasted_iota(jnp.int32, (FIRST_CHUNK, ROUTE_TILE), 0)
    for i in range(sub):
        base = pl.multiple_of(_floor_to(cs_ref[cs_base + i], 8), 8)
        prow = pos_ref[0, 0, :, i * ROUTE_TILE:(i + 1) * ROUTE_TILE]
        xt = h2_ref[0, i * ROUTE_TILE:(i + 1) * ROUTE_TILE, :]
        onehot = jnp.where(prow == base + jrow1, 1.0, 0.0).astype(BF16)
        xe_sc[g, pl.ds(base, FIRST_CHUNK), :] += jnp.dot(onehot, xt, preferred_element_type=F32)

    jrow = lax.broadcasted_iota(jnp.int32, (SLOT_CHUNK, ROUTE_TILE), 0)

    def tile_body(i, carry):
        first_end = _floor_to(cs_ref[cs_base + i], 8) + FIRST_CHUNK
        c1 = cs_ref[cs_base + i + 1]
        nch = jnp.where(c1 > first_end, _ceil_div(c1 - first_end, SLOT_CHUNK), 0)
        off = pl.multiple_of(i * ROUTE_TILE, ROUTE_TILE)

        def chunk(r, c):
            rb = pl.multiple_of(first_end + r * SLOT_CHUNK, 8)
            prow = pos_ref[0, 0, :, pl.ds(off, ROUTE_TILE)]
            onehot = jnp.where(prow == rb + jrow, 1.0, 0.0).astype(BF16)
            xe_sc[g, pl.ds(rb, SLOT_CHUNK), :] += jnp.dot(onehot, h2_ref[0, pl.ds(off, ROUTE_TILE), :],
                                                          preferred_element_type=F32)
            return c

        return lax.fori_loop(0, nch, chunk, carry)

    lax.fori_loop(0, sub, tile_body, 0)

    def expert(xb):
        a = jnp.dot(xb, wg_ref[0], preferred_element_type=F32)
        u = jnp.dot(xb, wu_ref[0], preferred_element_type=F32)
        hm = (a * _sigmoid(a) * u).astype(BF16)
        return jnp.dot(hm, wd_ref[0], preferred_element_type=F32).astype(BF16)

    @pl.when((s == ns - 1) & (g == group - 1))
    def _():
        if group == 1:
            for rc in range(cap // row_chunk):
                rows = slice(rc * row_chunk, (rc + 1) * row_chunk)
                ye_ref[0, 0, rows, :] = expert(xe_sc[0, rows, :].astype(BF16))
        else:
            y = expert(jnp.concatenate([xe_sc[i, 0:cap, :] for i in range(group)], axis=0).astype(BF16))
            for i in range(group):
                ye_ref[i, 0, 0:cap, :] = y[i * cap:(i + 1) * cap]
        for i in range(group):
            ye_ref[i, 0, cap:, :] = jnp.zeros((ye_ref.shape[2] - cap, ye_ref.shape[3]), BF16)


def _ffn(cs_flat, nt_pad, h2, pos4, wg, wu, wd):
    b, n, d = h2.shape
    ne, _, f = wg.shape
    cap = CAPACITY_FACTOR * n // N_EXPERTS
    rows = cap + SLOT_PAD
    tdma = min(n, 4096)
    row_chunk = min(cap, FFN_ROWS)
    group = b if b * cap <= FFN_ROWS else 1
    grid_spec = pltpu.PrefetchScalarGridSpec(
        num_scalar_prefetch=1,
        grid=(ne, b, n // tdma),
        in_specs=[pl.BlockSpec((1, tdma, d), lambda e, bi, s, cs: (bi, s, 0)),
                  pl.BlockSpec((1, 1, 1, tdma), lambda e, bi, s, cs: (bi, e, 0, s)),
                  pl.BlockSpec((1, d, f), lambda e, bi, s, cs: (e, 0, 0)),
                  pl.BlockSpec((1, d, f), lambda e, bi, s, cs: (e, 0, 0)),
                  pl.BlockSpec((1, f, d), lambda e, bi, s, cs: (e, 0, 0))],
        out_specs=pl.BlockSpec((group, 1, rows, d), lambda e, bi, s, cs: (bi // group, e, 0, 0)),
        scratch_shapes=[pltpu.VMEM((group, rows, d), F32)])
    return pl.pallas_call(
        functools.partial(_ffn_kernel, tdma=tdma, nt_pad=nt_pad, cap=cap, row_chunk=row_chunk, group=group),
        out_shape=jax.ShapeDtypeStruct((b, ne, rows, d), BF16),
        grid_spec=grid_spec,
        compiler_params=_params(("arbitrary", "arbitrary", "arbitrary")),
        name="ffn",
    )(cs_flat, h2, pos4, wg, wu, wd)


def _combine_kernel(cs_ref, pos_ref, g_ref, x_ref, mod_ref, gfin_ref, ye_hbm, out_ref,
                    ystage, pt_sc, acc_sc, sem, *, nt_pad, final):
    b = pl.program_id(0)
    i = pl.program_id(1)
    nt = pl.num_programs(1)
    step = b * nt + i
    slot = step % 2
    group = 4 * SLOT_CHUNK

    @pl.when(step == 0)
    def _():
        ystage[...] = jnp.zeros_like(ystage)

    def chunk_plan(bb, ii, e):
        c0 = cs_ref[(bb * N_EXPERTS + e) * nt_pad + ii]
        c1 = cs_ref[(bb * N_EXPERTS + e) * nt_pad + ii + 1]
        base = _floor_to(c0, 16)
        return base, jnp.where(c1 > c0, _ceil_div(c1 - base, SLOT_CHUNK), 0)

    def chunk_copy(bb, e, rb, sl, k):
        return pltpu.make_async_copy(ye_hbm.at[bb, e, pl.ds(rb, SLOT_CHUNK), :],
                                     ystage.at[sl, pl.ds(k * SLOT_CHUNK, SLOT_CHUNK), :], sem.at[sl])

    def issue(bb, ii, sl):
        k = jnp.int32(0)
        for e in range(N_EXPERTS):
            base, nch = chunk_plan(bb, ii, e)

            def start(r, kk, e=e, base=base):
                chunk_copy(bb, e, pl.multiple_of(base + r * SLOT_CHUNK, 16), sl, kk).start()
                return kk + 1

            k = lax.fori_loop(0, nch, start, k)

    @pl.when(step == 0)
    def _():
        issue(b, i, slot)

    @pl.when(step + 1 < pl.num_programs(0) * nt)
    def _():
        wrap = i + 1 == nt
        issue(jnp.where(wrap, b + 1, b), jnp.where(wrap, 0, i + 1), 1 - slot)

    jrow = lax.broadcasted_iota(jnp.int32, (SLOT_CHUNK, ROUTE_TILE), 0)
    k = jnp.int32(0)
    for e in range(N_EXPERTS):
        base, nch = chunk_plan(b, i, e)
        prow = pos_ref[0, e:e + 1, :]
        grow = g_ref[0, e:e + 1, :]

        def weights(r, kk, base=base, prow=prow, grow=grow):
            row0 = pl.multiple_of(kk * SLOT_CHUNK, SLOT_CHUNK)
            pt_sc[pl.ds(row0, SLOT_CHUNK), :] = jnp.where(prow == base + r * SLOT_CHUNK + jrow, grow, 0.0)
            return kk + 1

        k = lax.fori_loop(0, nch, weights, k)

    kpad = jnp.maximum(_ceil_div(k, 4) * 4, COMBINE_STATIC_CHUNKS)

    def zero_pad(kk, c):
        row0 = pl.multiple_of(kk * SLOT_CHUNK, SLOT_CHUNK)
        pt_sc[pl.ds(row0, SLOT_CHUNK), :] = jnp.zeros((SLOT_CHUNK, ROUTE_TILE), F32)
        return c

    lax.fori_loop(k, kpad, zero_pad, 0)

    def wait_one(kk, c):
        chunk_copy(0, 0, 0, slot, kk).wait()
        return c

    lax.fori_loop(0, k, wait_one, 0)

    static_rows = COMBINE_STATIC_CHUNKS * SLOT_CHUNK
    w = pt_sc[0:static_rows, :].T.astype(BF16)
    acc_sc[...] = jnp.dot(w, ystage[slot, 0:static_rows, :], preferred_element_type=F32)

    def matmul_group(gi, c):
        row0 = pl.multiple_of(gi * group, group)
        wg = pt_sc[pl.ds(row0, group), :].T.astype(BF16)
        acc_sc[...] += jnp.dot(wg, ystage[slot, pl.ds(row0, group), :], preferred_element_type=F32)
        return c

    lax.fori_loop(COMBINE_STATIC_CHUNKS // 4, kpad // 4, matmul_group, 0)

    xn = x_ref[0] + mod_ref[0][5:6] * acc_sc[...]
    if final:
        ms = jnp.mean(xn * xn, axis=-1, keepdims=True)
        xn = xn * lax.rsqrt(ms + EPS) * gfin_ref[...]
    out_ref[0] = xn


def _combine(cs_flat, nt_pad, pos, g, x, mod_l, mod_row, gfin, ye, final):
    b, n, d = x.shape
    per_sample = mod_row is None
    mod_map = ((lambda bi, i, cs: (bi, 0, 0)) if per_sample else (lambda bi, i, cs: (mod_row, 0, 0)))
    max_chunks = N_EXPERTS * ((ROUTE_TILE + 15 + SLOT_CHUNK - 1) // SLOT_CHUNK) + 4
    grid_spec = pltpu.PrefetchScalarGridSpec(
        num_scalar_prefetch=1,
        grid=(b, n // ROUTE_TILE),
        in_specs=[pl.BlockSpec((1, N_EXPERTS, ROUTE_TILE), lambda bi, i, cs: (bi, 0, i)),
                  pl.BlockSpec((1, N_EXPERTS, ROUTE_TILE), lambda bi, i, cs: (bi, 0, i)),
                  pl.BlockSpec((1, ROUTE_TILE, d), lambda bi, i, cs: (bi, i, 0)),
                  pl.BlockSpec((1, N_MOD, d), mod_map),
                  pl.BlockSpec((1, d), lambda bi, i, cs: (0, 0)),
                  pl.BlockSpec(memory_space=pl.ANY)],
        out_specs=pl.BlockSpec((1, ROUTE_TILE, d), lambda bi, i, cs: (bi, i, 0)),
        scratch_shapes=[pltpu.VMEM((2, max_chunks * SLOT_CHUNK, d), BF16),
                        pltpu.VMEM((max_chunks * SLOT_CHUNK, ROUTE_TILE), F32),
                        pltpu.VMEM((ROUTE_TILE, d), F32),
                        pltpu.SemaphoreType.DMA((2,))])
    return pl.pallas_call(
        functools.partial(_combine_kernel, nt_pad=nt_pad, final=final),
        out_shape=jax.ShapeDtypeStruct((b, n, d), F32),
        grid_spec=grid_spec,
        compiler_params=_params(("arbitrary", "arbitrary")),
        name="combine",
    )(cs_flat, pos, g, x, mod_l, gfin, ye)


def _moe(lt, h2, x, mod_l, mod_row, gfin, wg, wu, wd, final):
    b, n, _ = x.shape
    pos, g, cs = _route(lt)
    nt_pad = ((n // ROUTE_TILE + 1 + 7) // 8) * 8
    cs_flat = cs[:, :, :nt_pad].reshape(-1)
    ye = _ffn(cs_flat, nt_pad, h2, pos.reshape(b, N_EXPERTS, 1, n), wg, wu, wd)
    return _combine(cs_flat, nt_pad, pos, g, x, mod_l, mod_row, gfin, ye, final)


def kernel(x, c, ctx, c_ctx, w_mod, b_mod, g_norm1, g_norm2, w_in, w_out, conv_w, conv_b, gate_b, g_head,
           w_router, w_gate_e, w_up_e, w_down_e, g_final):
    bsz, n, d = x.shape
    nctx = ctx.shape[1]
    depth = w_mod.shape[0]
    dm = N_HEADS * HEAD_DIM
    dc = conv_w.shape[2]
    ng = 4 * N_HEADS
    ctx_row = bsz

    rows = ((bsz + 1 + 7) // 8) * 8
    cc = jnp.zeros((rows, d), F32).at[:bsz].set(c).at[bsz].set(c_ctx)
    mod = _modulation(cc, w_mod, b_mod).reshape(depth, rows, N_MOD, d)
    gfin = g_final.reshape(1, d)

    xl, xc = x, ctx
    for layer in range(depth):
        last = layer == depth - 1
        mod_l = mod[layer]
        wi = w_in[layer]
        wa = jnp.concatenate([wi[:, :3 * dc], wi[:, 3 * dc + 3 * dm:3 * dc + 4 * dm]], axis=1).astype(BF16)
        wb = wi[:, 3 * dc:3 * dc + 3 * dm].astype(BF16)
        wgt = jnp.zeros((d, LANES), BF16).at[:, :ng].set(wi[:, 3 * dc + 4 * dm:].astype(BF16))
        gb = jnp.zeros((1, LANES), F32).at[0, :ng].set(gate_b[layer].reshape(-1))
        gn1 = g_norm1[layer].reshape(1, d)
        gn2 = g_norm2[layer].reshape(1, d)
        cw = jnp.zeros((8, dc), F32).at[:conv_w.shape[1]].set(conv_w[layer])
        cb = conv_b[layer].reshape(1, dc)
        gh = g_head[layer].reshape(1, dm)
        wo = w_out[layer].astype(BF16)
        wr32 = jnp.zeros((d, LANES), F32).at[:, :N_EXPERTS].set(w_router[layer])
        wr_hi = wr32.astype(BF16)
        wr = jnp.concatenate([wr_hi, (wr32 - wr_hi.astype(F32)).astype(BF16), wr_hi], axis=0)
        wge = w_gate_e[layer].astype(BF16)
        wue = w_up_e[layer].astype(BF16)
        wde = w_down_e[layer].astype(BF16)

        convo_c, qk_c, vt_c, p_c, pt_c = _inproj(xc, mod_l, ctx_row, gn1, wa, wb, wgt, gb, tm=nctx)
        convo_l, qk_l, vt_l, p_l, pt_l = _inproj(xl, mod_l, None, gn1, wa, wb, wgt, gb, tm=512)
        zero_state = _mlstm_zero_state(bsz)
        hf_c, hb_c, state = _mlstm(qk_c, vt_c, p_c, pt_c, zero_state)
        hf_l, hb_l, _ = _mlstm(qk_l, vt_l, p_l, pt_l, state)

        xl, h2_l, lt_l = _mixout(convo_l, hf_l, hb_l, xl, mod_l, None, cw, cb, gh, wo, gn2, wr,
                                 tm=512, row_len=GRID_W)
        xl = _moe(lt_l, h2_l, xl, mod_l, None, gfin, wge, wue, wde, final=last)
        if not last:
            xc, h2_c, lt_c = _mixout(convo_c, hf_c, hb_c, xc, mod_l, ctx_row, cw, cb, gh, wo, gn2, wr,
                                     tm=nctx, row_len=nctx)
            xc = _moe(lt_c, h2_c, xc, mod_l, ctx_row, gfin, wge, wue, wde, final=False)
    return xl
```

```python
import functools

import jax
import jax.numpy as jnp
from jax import lax
from jax.experimental import pallas as pl
from jax.experimental.pallas import tpu as pltpu

EPS = 1e-6
N_HEADS = 4
HEAD_DIM = 128
CHUNK = 128
MLSTM_CHUNKS_PER_STEP = 4
N_EXPERTS = 16
CAPACITY_FACTOR = 2
N_MOD = 6
GRID_W = 64
LANES = 128
ROUTE_TILE = 256
SLOT_CHUNK = 64
FIRST_CHUNK = 64
FFN_ROWS = 512
SLOT_PAD = 128
COMBINE_STATIC_CHUNKS = 20
VMEM_LIMIT = 56 * 1024 * 1024

F32 = jnp.float32
BF16 = jnp.bfloat16
HIGHEST = lax.Precision.HIGHEST


def _params(sem, vmem=VMEM_LIMIT):
    return pltpu.CompilerParams(dimension_semantics=sem, vmem_limit_bytes=vmem)


def _sigmoid(x):
    return 1.0 / (1.0 + jnp.exp(-x))


def _floor_to(x, m):
    return jnp.bitwise_and(x, -m)


def _ceil_div(x, m):
    return jnp.right_shift(x + (m - 1), m.bit_length() - 1)


def _mod_kernel(c_ref, w_ref, b_ref, o_ref):
    c = c_ref[...]
    s = c * _sigmoid(c)
    o_ref[0] = jnp.dot(s, w_ref[0], precision=HIGHEST, preferred_element_type=F32) + b_ref[0]


def _modulation(cc, w_mod, b_mod):
    depth, d, nm = w_mod.shape
    rows = cc.shape[0]
    tn = 1536
    return pl.pallas_call(
        _mod_kernel,
        out_shape=jax.ShapeDtypeStruct((depth, rows, nm), F32),
        grid=(depth, nm // tn),
        in_specs=[pl.BlockSpec((rows, d), lambda l, j: (0, 0)),
                  pl.BlockSpec((1, d, tn), lambda l, j: (l, 0, j)),
                  pl.BlockSpec((1, 1, tn), lambda l, j: (l, 0, j))],
        out_specs=pl.BlockSpec((1, rows, tn), lambda l, j: (l, 0, j)),
        compiler_params=_params(("arbitrary", "arbitrary")),
        name="modulation",
    )(cc, w_mod, b_mod.reshape(depth, 1, nm))


def _inproj_kernel(x_ref, mod_ref, gn_ref, wa_ref, wb_ref, wg_ref, gb_ref,
                   convo_ref, qk_ref, vt_ref, p_ref, pt_ref, *, tm):
    x = x_ref[0]
    m = mod_ref[0]
    ms = jnp.mean(x * x, axis=-1, keepdims=True)
    h = x * lax.rsqrt(ms + EPS) * (gn_ref[...] * (1.0 + m[1:2])) + m[0:1]
    hb = h.astype(BF16)
    cw = 512
    for c in range(wa_ref.shape[1] // cw):
        convo_ref[0, :, c * cw:(c + 1) * cw] = jnp.dot(
            hb, wa_ref[:, c * cw:(c + 1) * cw], preferred_element_type=F32).astype(BF16)
    q = jnp.dot(hb, wb_ref[:, 0:cw], preferred_element_type=F32) * (HEAD_DIM ** -0.5)
    qk_ref[0, :, 0:cw] = q.astype(BF16)
    qk_ref[0, :, cw:2 * cw] = jnp.dot(hb, wb_ref[:, cw:2 * cw], preferred_element_type=F32).astype(BF16)
    v = jnp.dot(hb, wb_ref[:, 2 * cw:3 * cw], preferred_element_type=F32)
    vt_ref[0] = v.T.astype(BF16)

    g = jnp.dot(hb, wg_ref[...], preferred_element_type=F32) + gb_ref[...]
    kind = lax.broadcasted_iota(jnp.int32, (1, LANES), 1) // N_HEADS
    logf = jnp.minimum(g, 0.0) - jnp.log1p(jnp.exp(-jnp.abs(g)))
    g = jnp.where((kind == 1) | (kind == 3), logf, g)
    row = lax.broadcasted_iota(jnp.int32, (CHUNK, CHUNK), 0)
    col = lax.broadcasted_iota(jnp.int32, (CHUNK, CHUNK), 1)
    tri = jnp.concatenate([jnp.where(col <= row, 1.0, 0.0), jnp.where(col >= row, 1.0, 0.0)], axis=0).astype(BF16)
    for r in range(tm // CHUNK):
        gc = g[r * CHUNK:(r + 1) * CHUNK]
        g_hi = gc.astype(BF16)
        rem = gc - g_hi.astype(F32)
        g_mid = rem.astype(BF16)
        g_lo = (rem - g_mid.astype(F32)).astype(BF16)
        sums = jnp.dot(tri, jnp.concatenate([g_hi, g_mid, g_lo], axis=1), preferred_element_type=F32)
        sums = sums[:, 0:LANES] + sums[:, LANES:2 * LANES] + sums[:, 2 * LANES:3 * LANES]
        cum, suf = sums[0:CHUNK], sums[CHUNK:2 * CHUNK]
        p = jnp.where(kind == 1, cum, jnp.where(kind == 3, suf, gc))
        pt_ref[0, :, r * CHUNK:(r + 1) * CHUNK] = p.T[:4 * N_HEADS, :]
        diff = p - pltpu.roll(p, LANES - N_HEADS, axis=1)
        p_ref[0, r * CHUNK:(r + 1) * CHUNK, :] = diff[:, :4 * N_HEADS]


def _inproj(x, mod_l, mod_row, gn, wa, wb, wg, gb, tm):
    b, n, d = x.shape
    per_sample = mod_row is None
    mod_map = (lambda bi, i: (bi, 0, 0)) if per_sample else (lambda bi, i: (mod_row, 0, 0))
    ng = 4 * N_HEADS
    dm = N_HEADS * HEAD_DIM
    return pl.pallas_call(
        functools.partial(_inproj_kernel, tm=tm),
        out_shape=(jax.ShapeDtypeStruct((b, n, wa.shape[1]), BF16),
                   jax.ShapeDtypeStruct((b, n, 2 * dm), BF16),
                   jax.ShapeDtypeStruct((b, dm, n), BF16),
                   jax.ShapeDtypeStruct((b, n, ng), F32),
                   jax.ShapeDtypeStruct((b, ng, n), F32)),
        grid=(b, n // tm),
        in_specs=[pl.BlockSpec((1, tm, d), lambda bi, i: (bi, i, 0)),
                  pl.BlockSpec((1, N_MOD, d), mod_map),
                  pl.BlockSpec((1, d), lambda bi, i: (0, 0)),
                  pl.BlockSpec(wa.shape, lambda bi, i: (0, 0)),
                  pl.BlockSpec(wb.shape, lambda bi, i: (0, 0)),
                  pl.BlockSpec(wg.shape, lambda bi, i: (0, 0)),
                  pl.BlockSpec((1, LANES), lambda bi, i: (0, 0))],
        out_specs=(pl.BlockSpec((1, tm, wa.shape[1]), lambda bi, i: (bi, i, 0)),
                   pl.BlockSpec((1, tm, 2 * dm), lambda bi, i: (bi, i, 0)),
                   pl.BlockSpec((1, dm, tm), lambda bi, i: (bi, 0, i)),
                   pl.BlockSpec((1, tm, ng), lambda bi, i: (bi, i, 0)),
                   pl.BlockSpec((1, ng, tm), lambda bi, i: (bi, 0, i))),
        compiler_params=_params(("arbitrary", "arbitrary")),
        name="inproj",
    )(x, mod_l, gn, wa, wb, wg, gb)


def _mlstm_kernel(qf_ref, qb_ref, vf_ref, vb_ref, pf_ref, pb_ref, ptf_ref, ptb_ref, c0_ref, n0_ref, m0_ref,
                  hf_ref, hb_ref, c1_ref, n1_ref, m1_ref, c_sc, n_sc, m_sc, *, chunks):
    j = pl.program_id(1)
    nc = pl.num_programs(1)
    L = CHUNK
    dm = N_HEADS * HEAD_DIM
    pad = 16

    @pl.when(j == 0)
    def _():
        c_sc[...] = c0_ref[0]
        n_sc[...] = n0_ref[0]
        m_sc[...] = m0_ref[0]

    W = 2 * HEAD_DIM
    row = lax.broadcasted_iota(jnp.int32, (L, W), 0)
    col = lax.broadcasted_iota(jnp.int32, (L, W), 1) % L
    nt_dims = (((1,), (1,)), ((), ()))
    zeros = jnp.zeros((L, HEAD_DIM), BF16)
    pairs = [(d, pr) for d in range(2) for pr in range(N_HEADS // 2)]

    def blockdiag(x):
        return jnp.concatenate([jnp.concatenate([x[:, :HEAD_DIM], zeros], axis=1),
                                jnp.concatenate([zeros, x[:, HEAD_DIM:]], axis=1)], axis=0)

    def per_head(x, lane):
        return jnp.concatenate([jnp.broadcast_to(x[:, lane:lane + 1], (1, HEAD_DIM)),
                                jnp.broadcast_to(x[:, HEAD_DIM + lane:HEAD_DIM + lane + 1], (1, HEAD_DIM))], axis=1)

    def chunk_step(sub_f, sub_b):
        tok = (slice(sub_f * L, (sub_f + 1) * L), slice(sub_b * L, (sub_b + 1) * L))
        ks, vts, cts, nvs, r1 = [], [], [], [], []
        for d, pr in pairs:
            idx = d * (N_HEADS // 2) + pr
            qk_ref = (qf_ref, qb_ref)[d]
            q = qk_ref[0, tok[d], pr * W:(pr + 1) * W]
            k = qk_ref[0, tok[d], dm + pr * W:dm + (pr + 1) * W]
            ct, nvec = c_sc[idx], n_sc[idx]
            lhs = jnp.concatenate([k, ct.astype(BF16), jnp.broadcast_to(nvec, (pad, W)).astype(BF16)], axis=0)
            r1.append(lax.dot_general(lhs, blockdiag(q), nt_dims, preferred_element_type=F32))
            v_ref = (vf_ref, vb_ref)[d]
            vts.append(jnp.concatenate([v_ref[0, (2 * pr) * HEAD_DIM:(2 * pr + 1) * HEAD_DIM, tok[d]],
                                        v_ref[0, (2 * pr + 1) * HEAD_DIM:(2 * pr + 2) * HEAD_DIM, tok[d]]], axis=1))
            ks.append(k), cts.append(ct), nvs.append(nvec)

        w_intra, w_inter, floor, w_state, m_news, r3 = [], [], [], [], [], []
        for c, (d, pr) in enumerate(pairs):
            idx = d * (N_HEADS // 2) + pr
            ci = 2 * d * N_HEADS + 2 * pr
            p = (pf_ref, pb_ref)[d][0, tok[d], :]
            pt = (ptf_ref, ptb_ref)[d][0, :, tok[d]]
            g_col = jnp.concatenate([jnp.broadcast_to(p[:, ci:ci + 1], (L, HEAD_DIM)),
                                     jnp.broadcast_to(p[:, ci + 1:ci + 2], (L, HEAD_DIM))], axis=1)
            i_row = jnp.concatenate([pt[ci:ci + 1, :], pt[ci + 1:ci + 2, :]], axis=1)
            b_row = jnp.concatenate([pt[ci + N_HEADS:ci + N_HEADS + 1, :],
                                     pt[ci + N_HEADS + 1:ci + N_HEADS + 2, :]], axis=1)
            m_prev = m_sc[idx]
            mask = (row <= col) if d == 0 else (row >= col)
            last = L - 1 if d == 0 else 0
            dmat = jnp.where(mask, b_row + g_col, -jnp.inf)
            a = b_row + m_prev
            m_t = jnp.maximum(a, jnp.max(dmat, axis=0, keepdims=True))
            w_intra.append(jnp.exp(dmat - m_t))
            w_inter.append(jnp.exp(a - m_t))
            floor.append(jnp.exp(-m_t))
            m_new = per_head(m_t, last)
            b_last = per_head(b_row, last)
            w_state.append(jnp.exp(b_last + m_prev - m_new))
            m_news.append(m_new)
            w_tok = jnp.exp(b_last - b_row + i_row - m_new)
            lhs = jnp.concatenate([(vts[c].astype(F32) * w_tok).astype(BF16),
                                   jnp.broadcast_to(w_tok, (pad, W)).astype(BF16)], axis=0)
            r3.append(jnp.dot(lhs, blockdiag(ks[c]), preferred_element_type=F32))

        for c, (d, pr) in enumerate(pairs):
            s = r1[c][0:L] * w_intra[c]
            den = w_inter[c] * r1[c][2 * L:2 * L + 1] + jnp.sum(s, axis=0, keepdims=True)
            inv = 1.0 / jnp.maximum(jnp.abs(den), floor[c])
            ht = (jnp.dot(vts[c], blockdiag(s.astype(BF16)), preferred_element_type=F32)
                  + r1[c][L:2 * L] * w_inter[c]) * inv
            out_ref = (hf_ref, hb_ref)[d]
            out_ref[0, (2 * pr) * HEAD_DIM:(2 * pr + 1) * HEAD_DIM, tok[d]] = ht[:, :HEAD_DIM]
            out_ref[0, (2 * pr + 1) * HEAD_DIM:(2 * pr + 2) * HEAD_DIM, tok[d]] = ht[:, HEAD_DIM:]

        for c, (d, pr) in enumerate(pairs):
            idx = d * (N_HEADS // 2) + pr
            c_sc[idx] = w_state[c] * cts[c] + r3[c][0:HEAD_DIM]
            n_sc[idx] = w_state[c] * nvs[c] + r3[c][HEAD_DIM:HEAD_DIM + 1]
            m_sc[idx] = m_news[c]

    for u in range(chunks):
        chunk_step(u, chunks - 1 - u)

    @pl.when(j == nc - 1)
    def _():
        c1_ref[0] = c_sc[...]
        n1_ref[0] = n_sc[...]
        m1_ref[0] = m_sc[...]


def _mlstm_zero_state(bsz):
    npair = N_HEADS
    return (jnp.zeros((bsz, npair, HEAD_DIM, 2 * HEAD_DIM), F32),
            jnp.zeros((bsz, npair, 1, 2 * HEAD_DIM), F32),
            jnp.zeros((bsz, npair, 1, 2 * HEAD_DIM), F32))


def _mlstm(qk, vt, p, pt, state):
    b, n, w = qk.shape
    chunks = min(MLSTM_CHUNKS_PER_STEP, n // CHUNK)
    span = CHUNK * chunks
    nc = n // span
    ng = 4 * N_HEADS
    dm = N_HEADS * HEAD_DIM
    c0, n0, m0 = state
    fwd = lambda bi, j: (bi, j, 0)
    bwd = lambda bi, j: (bi, nc - 1 - j, 0)
    fwd_t = lambda bi, j: (bi, 0, j)
    bwd_t = lambda bi, j: (bi, 0, nc - 1 - j)
    st4 = lambda bi, j: (bi, 0, 0, 0)
    outs = pl.pallas_call(
        functools.partial(_mlstm_kernel, chunks=chunks),
        out_shape=(jax.ShapeDtypeStruct((b, dm, n), F32),
                   jax.ShapeDtypeStruct((b, dm, n), F32),
                   jax.ShapeDtypeStruct(c0.shape, F32),
                   jax.ShapeDtypeStruct(n0.shape, F32),
                   jax.ShapeDtypeStruct(m0.shape, F32)),
        grid=(b, nc),
        in_specs=[pl.BlockSpec((1, span, w), fwd),
                  pl.BlockSpec((1, span, w), bwd),
                  pl.BlockSpec((1, dm, span), fwd_t),
                  pl.BlockSpec((1, dm, span), bwd_t),
                  pl.BlockSpec((1, span, ng), fwd),
                  pl.BlockSpec((1, span, ng), bwd),
                  pl.BlockSpec((1, ng, span), fwd_t),
                  pl.BlockSpec((1, ng, span), bwd_t),
                  pl.BlockSpec((1,) + c0.shape[1:], st4),
                  pl.BlockSpec((1,) + n0.shape[1:], st4),
                  pl.BlockSpec((1,) + m0.shape[1:], st4)],
        out_specs=(pl.BlockSpec((1, dm, span), fwd_t),
                   pl.BlockSpec((1, dm, span), bwd_t),
                   pl.BlockSpec((1,) + c0.shape[1:], st4),
                   pl.BlockSpec((1,) + n0.shape[1:], st4),
                   pl.BlockSpec((1,) + m0.shape[1:], st4)),
        scratch_shapes=[pltpu.VMEM(c0.shape[1:], F32),
                        pltpu.VMEM(n0.shape[1:], F32),
                        pltpu.VMEM(m0.shape[1:], F32)],
        compiler_params=_params(("arbitrary", "arbitrary")),
        name="mlstm",
    )(qk, qk, vt, vt, p, p, pt, pt, c0, n0, m0)
    return outs[0], outs[1], (outs[2], outs[3], outs[4])


def _mixout_kernel(convo_ref, hf_ref, hb_ref, x_ref, mod_ref, cw_ref, cb_ref, gh_ref, wo_ref, gn2_ref, wr_ref,
                   xo_ref, h2_ref, lt_ref, *, tm, row_len):
    dc = cw_ref.shape[1]
    cw = cw_ref[...]
    m = mod_ref[0]
    gain2 = gn2_ref[...] * (1.0 + m[4:5])
    sub = LANES if LANES % row_len == 0 else tm
    t = lax.broadcasted_iota(jnp.int32, (sub, dc), 0) % row_len
    for r in range(tm // sub):
        rows = slice(r * sub, (r + 1) * sub)
        cv = convo_ref[0, rows, :]
        bg = cv[:, 0:dc].astype(F32)
        cg = cv[:, dc:2 * dc].astype(F32)
        xi = cv[:, 2 * dc:3 * dc].astype(F32)
        og = cv[:, 3 * dc:].astype(F32)
        u = cg * xi
        u_prev = jnp.where(t == 0, 0.0, pltpu.roll(u, 1, axis=0))
        u_next = jnp.where(t == row_len - 1, 0.0, pltpu.roll(u, sub - 1, axis=0))
        yc = bg * (cw[0:1] * u_prev + cw[1:2] * u + cw[2:3] * u_next + cb_ref[...])

        hm = hf_ref[0, :, rows] + hb_ref[0, :, rows]
        parts = []
        for hd in range(N_HEADS):
            hh = hm[hd * HEAD_DIM:(hd + 1) * HEAD_DIM, :]
            parts.append(hh * lax.rsqrt(jnp.mean(hh * hh, axis=0, keepdims=True) + EPS))
        ym = _sigmoid(og) * (jnp.concatenate(parts, axis=0).T * gh_ref[...])

        cat = jnp.concatenate([yc, ym], axis=-1).astype(BF16)
        y = jnp.dot(cat, wo_ref[...], preferred_element_type=F32)
        xn = x_ref[0, rows, :] + m[2:3] * y
        xo_ref[0, rows, :] = xn
        ms = jnp.mean(xn * xn, axis=-1, keepdims=True)
        h2 = xn * lax.rsqrt(ms + EPS) * gain2 + m[3:4]
        hi = h2.astype(BF16)
        h2_ref[0, rows, :] = hi
        lo = (h2 - hi.astype(F32)).astype(BF16)
        lg = jnp.dot(jnp.concatenate([hi, hi, lo], axis=1), wr_ref[...], preferred_element_type=F32)
        for q in range(sub // LANES):
            lanes = slice(r * sub + q * LANES, r * sub + (q + 1) * LANES)
            lt_ref[0, :, lanes] = lg[q * LANES:(q + 1) * LANES, :].T[:N_EXPERTS, :]


def _mixout(convo, hf, hb, x, mod_l, mod_row, cw, cb, gh, wo, gn2, wr, tm, row_len):
    b, n, d = x.shape
    per_sample = mod_row is None
    mod_map = (lambda bi, i: (bi, 0, 0)) if per_sample else (lambda bi, i: (mod_row, 0, 0))
    tok = lambda bi, i: (bi, i, 0)
    cst = lambda bi, i: (0, 0)
    return pl.pallas_call(
        functools.partial(_mixout_kernel, tm=tm, row_len=row_len),
        out_shape=(jax.ShapeDtypeStruct((b, n, d), F32),
                   jax.ShapeDtypeStruct((b, n, d), BF16),
                   jax.ShapeDtypeStruct((b, N_EXPERTS, n), F32)),
        grid=(b, n // tm),
        in_specs=[pl.BlockSpec((1, tm, convo.shape[2]), tok),
                  pl.BlockSpec((1, hf.shape[1], tm), lambda bi, i: (bi, 0, i)),
                  pl.BlockSpec((1, hb.shape[1], tm), lambda bi, i: (bi, 0, i)),
                  pl.BlockSpec((1, tm, d), tok),
                  pl.BlockSpec((1, N_MOD, d), mod_map),
                  pl.BlockSpec(cw.shape, cst),
                  pl.BlockSpec(cb.shape, cst),
                  pl.BlockSpec(gh.shape, cst),
                  pl.BlockSpec(wo.shape, cst),
                  pl.BlockSpec(gn2.shape, cst),
                  pl.BlockSpec(wr.shape, cst)],
        out_specs=(pl.BlockSpec((1, tm, d), tok),
                   pl.BlockSpec((1, tm, d), tok),
                   pl.BlockSpec((1, N_EXPERTS, tm), lambda bi, i: (bi, 0, i))),
        compiler_params=_params(("arbitrary", "arbitrary")),
        name="mixout",
    )(convo, hf, hb, x, mod_l, cw, cb, gh, wo, gn2, wr)


def _route_kernel(lt_ref, pos_ref, g_ref, cs_ref, aff_sc, sel_sc, *, n, cap):
    lg = lt_ref[0]
    e = jnp.exp(lg - jnp.max(lg, axis=0, keepdims=True))
    aff_sc[...] = e / jnp.sum(e, axis=0, keepdims=True)
    capf = jnp.float32(cap)

    def count_ge(cand):
        acc = jnp.zeros((N_EXPERTS, LANES), F32)
        for j in range(n // LANES):
            acc = acc + jnp.where(aff_sc[:, j * LANES:(j + 1) * LANES] >= cand, 1.0, 0.0)
        return jnp.sum(acc, axis=1, keepdims=True)

    def exp_step(_, c):
        elo, ehi = c
        emid = jnp.floor((elo + ehi) * 0.5)
        ok = count_ge(jnp.exp2(emid)) >= capf
        return jnp.where(ok, emid, elo), jnp.where(ok, ehi, emid)

    elo, ehi = lax.fori_loop(0, 8, exp_step, (jnp.full((N_EXPERTS, 1), -128.0, F32), jnp.ones((N_EXPERTS, 1), F32)))

    def val_step(_, c):
        lo, hi = c
        mid = lo + (hi - lo) * 0.5
        ok = count_ge(mid) >= capf
        return jnp.where(ok, mid, lo), jnp.where(ok, hi, mid)

    lo, hi = lax.fori_loop(0, 40, val_step, (jnp.where(elo <= -128.0, 0.0, jnp.exp2(elo)), jnp.exp2(ehi)))
    need = capf - count_ge(hi)

    row = lax.broadcasted_iota(jnp.int32, (LANES, LANES), 0)
    col = lax.broadcasted_iota(jnp.int32, (LANES, LANES), 1)
    upper = jnp.where(row < col, 1.0, 0.0).astype(BF16)
    lane = lax.broadcasted_iota(jnp.int32, (N_EXPERTS, LANES), 1)
    per_tile = ROUTE_TILE // LANES

    ceq = jnp.zeros((N_EXPERTS, 1), F32)
    for c in range(n // LANES):
        a = aff_sc[:, c * LANES:(c + 1) * LANES]
        tie = (a >= lo) & (a < hi)
        eq = jnp.where(tie, 1.0, 0.0)
        peq = jnp.dot(eq.astype(BF16), upper, preferred_element_type=F32) + ceq
        sel = (a >= hi) | (tie & (peq < need))
        g_ref[0, :, c * LANES:(c + 1) * LANES] = jnp.where(sel, a, 0.0)
        sel_sc[:, c * LANES:(c + 1) * LANES] = jnp.where(sel, 1.0, 0.0)
        ceq = ceq + jnp.sum(eq, axis=1, keepdims=True)

    csel = jnp.zeros((N_EXPERTS, 1), F32)
    offs = jnp.zeros((N_EXPERTS, LANES), jnp.int32)
    for c in range(n // LANES):
        self = sel_sc[:, c * LANES:(c + 1) * LANES]
        psel = jnp.dot(self.astype(BF16), upper, preferred_element_type=F32) + csel
        pos_ref[0, :, c * LANES:(c + 1) * LANES] = jnp.where(self > 0.0, psel.astype(jnp.int32), -1)
        if c % per_tile == 0:
            offs = jnp.where(lane == c // per_tile, csel.astype(jnp.int32), offs)
        csel = csel + jnp.sum(self, axis=1, keepdims=True)
    cs_ref[0] = jnp.where(lane >= n // ROUTE_TILE, csel.astype(jnp.int32), offs)


def _route(lt):
    b, ne, n = lt.shape
    cap = CAPACITY_FACTOR * n // N_EXPERTS
    blk = pl.BlockSpec((1, ne, n), lambda bi: (bi, 0, 0))
    return pl.pallas_call(
        functools.partial(_route_kernel, n=n, cap=cap),
        out_shape=(jax.ShapeDtypeStruct((b, ne, n), jnp.int32),
                   jax.ShapeDtypeStruct((b, ne, n), F32),
                   jax.ShapeDtypeStruct((b, ne, LANES), jnp.int32)),
        grid=(b,),
        in_specs=[blk],
        out_specs=(blk, blk, pl.BlockSpec((1, ne, LANES), lambda bi: (bi, 0, 0))),
        scratch_shapes=[pltpu.VMEM((ne, n), F32), pltpu.VMEM((ne, n), F32)],
        compiler_params=_params(("arbitrary",)),
        name="route",
    )(lt)


def _ffn_kernel(cs_ref, h2_ref, pos_ref, wg_ref, wu_ref, wd_ref, ye_ref, xe_sc, *, tdma, nt_pad, cap, row_chunk,
                group):
    e = pl.program_id(0)
    b = pl.program_id(1)
    s = pl.program_id(2)
    ns = pl.num_programs(2)
    sub = tdma // ROUTE_TILE
    g = b % group

    @pl.when(s == 0)
    def _():
        xe_sc[g] = jnp.zeros(xe_sc.shape[1:], F32)

    cs_base = (b * N_EXPERTS + e) * nt_pad + s * sub

    jrow1 = lax.broadcasted_iota(jnp.int32, (FIRST_CHUNK, ROUTE_TILE), 0)
    for i in range(sub):
        base = pl.multiple_of(_floor_to(cs_ref[cs_base + i], 8), 8)
        prow = pos_ref[0, 0, :, i * ROUTE_TILE:(i + 1) * ROUTE_TILE]
        xt = h2_ref[0, i * ROUTE_TILE:(i + 1) * ROUTE_TILE, :]
        onehot = jnp.where(prow == base + jrow1, 1.0, 0.0).astype(BF16)
        xe_sc[g, pl.ds(base, FIRST_CHUNK), :] += jnp.dot(onehot, xt, preferred_element_type=F32)

    jrow = lax.broadcasted_iota(jnp.int32, (SLOT_CHUNK, ROUTE_TILE), 0)

    def tile_body(i, carry):
        first_end = _floor_to(cs_ref[cs_base + i], 8) + FIRST_CHUNK
        c1 = cs_ref[cs_base + i + 1]
        nch = jnp.where(c1 > first_end, _ceil_div(c1 - first_end, SLOT_CHUNK), 0)
        off = pl.multiple_of(i * ROUTE_TILE, ROUTE_TILE)

        def chunk(r, c):
            rb = pl.multiple_of(first_end + r * SLOT_CHUNK, 8)
            prow = pos_ref[0, 0, :, pl.ds(off, ROUTE_TILE)]
            onehot = jnp.where(prow == rb + jrow, 1.0, 0.0).astype(BF16)
            xe_sc[g, pl.ds(rb, SLOT_CHUNK), :] += jnp.dot(onehot, h2_ref[0, pl.ds(off, ROUTE_TILE), :],
                                                          preferred_element_type=F32)
            return c

        return lax.fori_loop(0, nch, chunk, carry)

    lax.fori_loop(0, sub, tile_body, 0)

    def expert(xb):
        a = jnp.dot(xb, wg_ref[0], preferred_element_type=F32)
        u = jnp.dot(xb, wu_ref[0], preferred_element_type=F32)
        hm = (a * _sigmoid(a) * u).astype(BF16)
        return jnp.dot(hm, wd_ref[0], preferred_element_type=F32).astype(BF16)

    @pl.when((s == ns - 1) & (g == group - 1))
    def _():
        if group == 1:
            for rc in range(cap // row_chunk):
                rows = slice(rc * row_chunk, (rc + 1) * row_chunk)
                ye_ref[0, 0, rows, :] = expert(xe_sc[0, rows, :].astype(BF16))
        else:
            y = expert(jnp.concatenate([xe_sc[i, 0:cap, :] for i in range(group)], axis=0).astype(BF16))
            for i in range(group):
                ye_ref[i, 0, 0:cap, :] = y[i * cap:(i + 1) * cap]
        for i in range(group):
            ye_ref[i, 0, cap:, :] = jnp.zeros((ye_ref.shape[2] - cap, ye_ref.shape[3]), BF16)


def _ffn(cs_flat, nt_pad, h2, pos4, wg, wu, wd):
    b, n, d = h2.shape
    ne, _, f = wg.shape
    cap = CAPACITY_FACTOR * n // N_EXPERTS
    rows = cap + SLOT_PAD
    tdma = min(n, 4096)
    row_chunk = min(cap, FFN_ROWS)
    group = b if b * cap <= FFN_ROWS else 1
    grid_spec = pltpu.PrefetchScalarGridSpec(
        num_scalar_prefetch=1,
        grid=(ne, b, n // tdma),
        in_specs=[pl.BlockSpec((1, tdma, d), lambda e, bi, s, cs: (bi, s, 0)),
                  pl.BlockSpec((1, 1, 1, tdma), lambda e, bi, s, cs: (bi, e, 0, s)),
                  pl.BlockSpec((1, d, f), lambda e, bi, s, cs: (e, 0, 0)),
                  pl.BlockSpec((1, d, f), lambda e, bi, s, cs: (e, 0, 0)),
                  pl.BlockSpec((1, f, d), lambda e, bi, s, cs: (e, 0, 0))],
        out_specs=pl.BlockSpec((group, 1, rows, d), lambda e, bi, s, cs: (bi // group, e, 0, 0)),
        scratch_shapes=[pltpu.VMEM((group, rows, d), F32)])
    return pl.pallas_call(
        functools.partial(_ffn_kernel, tdma=tdma, nt_pad=nt_pad, cap=cap, row_chunk=row_chunk, group=group),
        out_shape=jax.ShapeDtypeStruct((b, ne, rows, d), BF16),
        grid_spec=grid_spec,
        compiler_params=_params(("arbitrary", "arbitrary", "arbitrary")),
        name="ffn",
    )(cs_flat, h2, pos4, wg, wu, wd)


def _combine_kernel(cs_ref, pos_ref, g_ref, x_ref, mod_ref, gfin_ref, ye_hbm, out_ref,
                    ystage, pt_sc, acc_sc, sem, *, nt_pad, final):
    b = pl.program_id(0)
    i = pl.program_id(1)
    nt = pl.num_programs(1)
    step = b * nt + i
    slot = step % 2
    group = 4 * SLOT_CHUNK

    @pl.when(step == 0)
    def _():
        ystage[...] = jnp.zeros_like(ystage)

    def chunk_plan(bb, ii, e):
        c0 = cs_ref[(bb * N_EXPERTS + e) * nt_pad + ii]
        c1 = cs_ref[(bb * N_EXPERTS + e) * nt_pad + ii + 1]
        base = _floor_to(c0, 16)
        return base, jnp.where(c1 > c0, _ceil_div(c1 - base, SLOT_CHUNK), 0)

    def chunk_copy(bb, e, rb, sl, k):
        return pltpu.make_async_copy(ye_hbm.at[bb, e, pl.ds(rb, SLOT_CHUNK), :],
                                     ystage.at[sl, pl.ds(k * SLOT_CHUNK, SLOT_CHUNK), :], sem.at[sl])

    def extra_chunks(plans):
        return sum(jnp.maximum(nch - 1, 0) for _, nch in plans)

    def issue(bb, ii, sl):
        plans = [chunk_plan(bb, ii, e) for e in range(N_EXPERTS)]
        for e, (base, _) in enumerate(plans):
            chunk_copy(bb, e, pl.multiple_of(base, 16), sl, e).start()

        @pl.when(extra_chunks(plans) > 0)
        def _():
            k = jnp.int32(N_EXPERTS)
            for e, (base, nch) in enumerate(plans):
                def start(r, kk, e=e, base=base):
                    chunk_copy(bb, e, pl.multiple_of(base + r * SLOT_CHUNK, 16), sl, kk).start()
                    return kk + 1

                k = lax.fori_loop(1, nch, start, k)

    @pl.when(step == 0)
    def _():
        issue(b, i, slot)

    @pl.when(step + 1 < pl.num_programs(0) * nt)
    def _():
        wrap = i + 1 == nt
        issue(jnp.where(wrap, b + 1, b), jnp.where(wrap, 0, i + 1), 1 - slot)

    jrow = lax.broadcasted_iota(jnp.int32, (SLOT_CHUNK, ROUTE_TILE), 0)
    plans = [chunk_plan(b, i, e) for e in range(N_EXPERTS)]
    for e, (base, _) in enumerate(plans):
        pt_sc[e * SLOT_CHUNK:(e + 1) * SLOT_CHUNK, :] = jnp.where(
            pos_ref[0, e:e + 1, :] == base + jrow, g_ref[0, e:e + 1, :], 0.0)
    k = N_EXPERTS + extra_chunks(plans)

    @pl.when(k > N_EXPERTS)
    def _():
        kk0 = jnp.int32(N_EXPERTS)
        for e, (base, nch) in enumerate(plans):
            def weights(r, kk, e=e, base=base):
                row0 = pl.multiple_of(kk * SLOT_CHUNK, SLOT_CHUNK)
                pt_sc[pl.ds(row0, SLOT_CHUNK), :] = jnp.where(
                    pos_ref[0, e:e + 1, :] == base + r * SLOT_CHUNK + jrow, g_ref[0, e:e + 1, :], 0.0)
                return kk + 1

            kk0 = lax.fori_loop(1, nch, weights, kk0)

    kpad = jnp.maximum(_ceil_div(k, 4) * 4, COMBINE_STATIC_CHUNKS)

    def zero_pad(kk, c):
        row0 = pl.multiple_of(kk * SLOT_CHUNK, SLOT_CHUNK)
        pt_sc[pl.ds(row0, SLOT_CHUNK), :] = jnp.zeros((SLOT_CHUNK, ROUTE_TILE), F32)
        return c

    lax.fori_loop(k, kpad, zero_pad, 0)

    def wait_one(kk, c):
        chunk_copy(0, 0, 0, slot, kk).wait()
        return c

    lax.fori_loop(0, k, wait_one, 0)

    static_rows = COMBINE_STATIC_CHUNKS * SLOT_CHUNK
    w = pt_sc[0:static_rows, :].T.astype(BF16)
    acc_sc[...] = jnp.dot(w, ystage[slot, 0:static_rows, :], preferred_element_type=F32)

    def matmul_group(gi, c):
        row0 = pl.multiple_of(gi * group, group)
        wg = pt_sc[pl.ds(row0, group), :].T.astype(BF16)
        acc_sc[...] += jnp.dot(wg, ystage[slot, pl.ds(row0, group), :], preferred_element_type=F32)
        return c

    lax.fori_loop(COMBINE_STATIC_CHUNKS // 4, kpad // 4, matmul_group, 0)

    xn = x_ref[0] + mod_ref[0][5:6] * acc_sc[...]
    if final:
        ms = jnp.mean(xn * xn, axis=-1, keepdims=True)
        xn = xn * lax.rsqrt(ms + EPS) * gfin_ref[...]
    out_ref[0] = xn


def _combine(cs_flat, nt_pad, pos, g, x, mod_l, mod_row, gfin, ye, final):
    b, n, d = x.shape
    per_sample = mod_row is None
    mod_map = ((lambda bi, i, cs: (bi, 0, 0)) if per_sample else (lambda bi, i, cs: (mod_row, 0, 0)))
    max_chunks = N_EXPERTS * ((ROUTE_TILE + 15 + SLOT_CHUNK - 1) // SLOT_CHUNK) + 4
    grid_spec = pltpu.PrefetchScalarGridSpec(
        num_scalar_prefetch=1,
        grid=(b, n // ROUTE_TILE),
        in_specs=[pl.BlockSpec((1, N_EXPERTS, ROUTE_TILE), lambda bi, i, cs: (bi, 0, i)),
                  pl.BlockSpec((1, N_EXPERTS, ROUTE_TILE), lambda bi, i, cs: (bi, 0, i)),
                  pl.BlockSpec((1, ROUTE_TILE, d), lambda bi, i, cs: (bi, i, 0)),
                  pl.BlockSpec((1, N_MOD, d), mod_map),
                  pl.BlockSpec((1, d), lambda bi, i, cs: (0, 0)),
                  pl.BlockSpec(memory_space=pl.ANY)],
        out_specs=pl.BlockSpec((1, ROUTE_TILE, d), lambda bi, i, cs: (bi, i, 0)),
        scratch_shapes=[pltpu.VMEM((2, max_chunks * SLOT_CHUNK, d), BF16),
                        pltpu.VMEM((max_chunks * SLOT_CHUNK, ROUTE_TILE), F32),
                        pltpu.VMEM((ROUTE_TILE, d), F32),
                        pltpu.SemaphoreType.DMA((2,))])
    return pl.pallas_call(
        functools.partial(_combine_kernel, nt_pad=nt_pad, final=final),
        out_shape=jax.ShapeDtypeStruct((b, n, d), F32),
        grid_spec=grid_spec,
        compiler_params=_params(("arbitrary", "arbitrary")),
        name="combine",
    )(cs_flat, pos, g, x, mod_l, gfin, ye)


def _moe(lt, h2, x, mod_l, mod_row, gfin, wg, wu, wd, final):
    b, n, _ = x.shape
    pos, g, cs = _route(lt)
    nt_pad = ((n // ROUTE_TILE + 1 + 7) // 8) * 8
    cs_flat = cs[:, :, :nt_pad].reshape(-1)
    ye = _ffn(cs_flat, nt_pad, h2, pos.reshape(b, N_EXPERTS, 1, n), wg, wu, wd)
    return _combine(cs_flat, nt_pad, pos, g, x, mod_l, mod_row, gfin, ye, final)


def kernel(x, c, ctx, c_ctx, w_mod, b_mod, g_norm1, g_norm2, w_in, w_out, conv_w, conv_b, gate_b, g_head,
           w_router, w_gate_e, w_up_e, w_down_e, g_final):
    bsz, n, d = x.shape
    nctx = ctx.shape[1]
    depth = w_mod.shape[0]
    dm = N_HEADS * HEAD_DIM
    dc = conv_w.shape[2]
    ng = 4 * N_HEADS
    ctx_row = bsz

    rows = ((bsz + 1 + 7) // 8) * 8
    cc = jnp.zeros((rows, d), F32).at[:bsz].set(c).at[bsz].set(c_ctx)
    mod = _modulation(cc, w_mod, b_mod).reshape(depth, rows, N_MOD, d)
    gfin = g_final.reshape(1, d)

    xl, xc = x, ctx
    for layer in range(depth):
        last = layer == depth - 1
        mod_l = mod[layer]
        wi = w_in[layer]
        wa = jnp.concatenate([wi[:, :3 * dc], wi[:, 3 * dc + 3 * dm:3 * dc + 4 * dm]], axis=1).astype(BF16)
        wb = wi[:, 3 * dc:3 * dc + 3 * dm].astype(BF16)
        wgt = jnp.zeros((d, LANES), BF16).at[:, :ng].set(wi[:, 3 * dc + 4 * dm:].astype(BF16))
        gb = jnp.zeros((1, LANES), F32).at[0, :ng].set(gate_b[layer].reshape(-1))
        gn1 = g_norm1[layer].reshape(1, d)
        gn2 = g_norm2[layer].reshape(1, d)
        cw = jnp.zeros((8, dc), F32).at[:conv_w.shape[1]].set(conv_w[layer])
        cb = conv_b[layer].reshape(1, dc)
        gh = g_head[layer].reshape(1, dm)
        wo = w_out[layer].astype(BF16)
        wr32 = jnp.zeros((d, LANES), F32).at[:, :N_EXPERTS].set(w_router[layer])
        wr_hi = wr32.astype(BF16)
        wr = jnp.concatenate([wr_hi, (wr32 - wr_hi.astype(F32)).astype(BF16), wr_hi], axis=0)
        wge = w_gate_e[layer].astype(BF16)
        wue = w_up_e[layer].astype(BF16)
        wde = w_down_e[layer].astype(BF16)

        convo_c, qk_c, vt_c, p_c, pt_c = _inproj(xc, mod_l, ctx_row, gn1, wa, wb, wgt, gb, tm=nctx)
        convo_l, qk_l, vt_l, p_l, pt_l = _inproj(xl, mod_l, None, gn1, wa, wb, wgt, gb, tm=512)
        zero_state = _mlstm_zero_state(bsz)
        hf_c, hb_c, state = _mlstm(qk_c, vt_c, p_c, pt_c, zero_state)
        hf_l, hb_l, _ = _mlstm(qk_l, vt_l, p_l, pt_l, state)

        xl, h2_l, lt_l = _mixout(convo_l, hf_l, hb_l, xl, mod_l, None, cw, cb, gh, wo, gn2, wr,
                                 tm=512, row_len=GRID_W)
        xl = _moe(lt_l, h2_l, xl, mod_l, None, gfin, wge, wue, wde, final=last)
        if not last:
            xc, h2_c, lt_c = _mixout(convo_c, hf_c, hb_c, xc, mod_l, ctx_row, cw, cb, gh, wo, gn2, wr,
                                     tm=nctx, row_len=nctx)
            xc = _moe(lt_c, h2_c, xc, mod_l, ctx_row, gfin, wge, wue, wde, final=False)
    return xl
```

```python
import functools

import jax
import jax.numpy as jnp
from jax import lax
from jax.experimental import pallas as pl
from jax.experimental.pallas import tpu as pltpu

EPS = 1e-6
N_HEADS = 4
HEAD_DIM = 128
CHUNK = 128
MLSTM_CHUNKS_PER_STEP = 4
N_EXPERTS = 16
CAPACITY_FACTOR = 2
N_MOD = 6
GRID_W = 64
LANES = 128
ROUTE_TILE = 256
SLOT_CHUNK = 64
FIRST_CHUNK = 64
FFN_ROWS = 512
FFN_EARLY_ROWS = 384
SLOT_PAD = 128
COMBINE_STATIC_CHUNKS = 20
VMEM_LIMIT = 56 * 1024 * 1024

F32 = jnp.float32
BF16 = jnp.bfloat16
HIGHEST = lax.Precision.HIGHEST


def _params(sem, vmem=VMEM_LIMIT):
    return pltpu.CompilerParams(dimension_semantics=sem, vmem_limit_bytes=vmem)


def _sigmoid(x):
    return 1.0 / (1.0 + jnp.exp(-x))


def _floor_to(x, m):
    return jnp.bitwise_and(x, -m)


def _ceil_div(x, m):
    return jnp.right_shift(x + (m - 1), m.bit_length() - 1)


def _mod_kernel(c_ref, w_ref, b_ref, o_ref):
    c = c_ref[...]
    s = c * _sigmoid(c)
    o_ref[0] = jnp.dot(s, w_ref[0], precision=HIGHEST, preferred_element_type=F32) + b_ref[0]


def _modulation(cc, w_mod, b_mod):
    depth, d, nm = w_mod.shape
    rows = cc.shape[0]
    tn = 1536
    return pl.pallas_call(
        _mod_kernel,
        out_shape=jax.ShapeDtypeStruct((depth, rows, nm), F32),
        grid=(depth, nm // tn),
        in_specs=[pl.BlockSpec((rows, d), lambda l, j: (0, 0)),
                  pl.BlockSpec((1, d, tn), lambda l, j: (l, 0, j)),
                  pl.BlockSpec((1, 1, tn), lambda l, j: (l, 0, j))],
        out_specs=pl.BlockSpec((1, rows, tn), lambda l, j: (l, 0, j)),
        compiler_params=_params(("arbitrary", "arbitrary")),
        name="modulation",
    )(cc, w_mod, b_mod.reshape(depth, 1, nm))


def _inproj_kernel(x_ref, mod_ref, gn_ref, wa_ref, wb_ref, wg_ref, gb_ref,
                   convo_ref, qk_ref, vt_ref, p_ref, pt_ref, *, tm):
    x = x_ref[0]
    m = mod_ref[0]
    ms = jnp.mean(x * x, axis=-1, keepdims=True)
    h = x * lax.rsqrt(ms + EPS) * (gn_ref[...] * (1.0 + m[1:2])) + m[0:1]
    hb = h.astype(BF16)
    cw = 512
    for c in range(wa_ref.shape[1] // cw):
        convo_ref[0, :, c * cw:(c + 1) * cw] = jnp.dot(
            hb, wa_ref[:, c * cw:(c + 1) * cw], preferred_element_type=F32).astype(BF16)
    q = jnp.dot(hb, wb_ref[:, 0:cw], preferred_element_type=F32) * (HEAD_DIM ** -0.5)
    qk_ref[0, :, 0:cw] = q.astype(BF16)
    qk_ref[0, :, cw:2 * cw] = jnp.dot(hb, wb_ref[:, cw:2 * cw], preferred_element_type=F32).astype(BF16)
    v = jnp.dot(hb, wb_ref[:, 2 * cw:3 * cw], preferred_element_type=F32)
    vt_ref[0] = v.T.astype(BF16)

    g = jnp.dot(hb, wg_ref[...], preferred_element_type=F32) + gb_ref[...]
    kind = lax.broadcasted_iota(jnp.int32, (1, LANES), 1) // N_HEADS
    logf = jnp.minimum(g, 0.0) - jnp.log1p(jnp.exp(-jnp.abs(g)))
    g = jnp.where((kind == 1) | (kind == 3), logf, g)
    row = lax.broadcasted_iota(jnp.int32, (CHUNK, CHUNK), 0)
    col = lax.broadcasted_iota(jnp.int32, (CHUNK, CHUNK), 1)
    tri = jnp.concatenate([jnp.where(col <= row, 1.0, 0.0), jnp.where(col >= row, 1.0, 0.0)], axis=0).astype(BF16)
    for r in range(tm // CHUNK):
        gc = g[r * CHUNK:(r + 1) * CHUNK]
        g_hi = gc.astype(BF16)
        rem = gc - g_hi.astype(F32)
        g_mid = rem.astype(BF16)
        g_lo = (rem - g_mid.astype(F32)).astype(BF16)
        sums = jnp.dot(tri, jnp.concatenate([g_hi, g_mid, g_lo], axis=1), preferred_element_type=F32)
        sums = sums[:, 0:LANES] + sums[:, LANES:2 * LANES] + sums[:, 2 * LANES:3 * LANES]
        cum, suf = sums[0:CHUNK], sums[CHUNK:2 * CHUNK]
        p = jnp.where(kind == 1, cum, jnp.where(kind == 3, suf, gc))
        pt_ref[0, :, r * CHUNK:(r + 1) * CHUNK] = p.T[:4 * N_HEADS, :]
        diff = p - pltpu.roll(p, LANES - N_HEADS, axis=1)
        p_ref[0, r * CHUNK:(r + 1) * CHUNK, :] = diff[:, :4 * N_HEADS]


def _inproj(x, mod_l, mod_row, gn, wa, wb, wg, gb, tm):
    b, n, d = x.shape
    per_sample = mod_row is None
    mod_map = (lambda bi, i: (bi, 0, 0)) if per_sample else (lambda bi, i: (mod_row, 0, 0))
    ng = 4 * N_HEADS
    dm = N_HEADS * HEAD_DIM
    return pl.pallas_call(
        functools.partial(_inproj_kernel, tm=tm),
        out_shape=(jax.ShapeDtypeStruct((b, n, wa.shape[1]), BF16),
                   jax.ShapeDtypeStruct((b, n, 2 * dm), BF16),
                   jax.ShapeDtypeStruct((b, dm, n), BF16),
                   jax.ShapeDtypeStruct((b, n, ng), F32),
                   jax.ShapeDtypeStruct((b, ng, n), F32)),
        grid=(b, n // tm),
        in_specs=[pl.BlockSpec((1, tm, d), lambda bi, i: (bi, i, 0)),
                  pl.BlockSpec((1, N_MOD, d), mod_map),
                  pl.BlockSpec((1, d), lambda bi, i: (0, 0)),
                  pl.BlockSpec(wa.shape, lambda bi, i: (0, 0)),
                  pl.BlockSpec(wb.shape, lambda bi, i: (0, 0)),
                  pl.BlockSpec(wg.shape, lambda bi, i: (0, 0)),
                  pl.BlockSpec((1, LANES), lambda bi, i: (0, 0))],
        out_specs=(pl.BlockSpec((1, tm, wa.shape[1]), lambda bi, i: (bi, i, 0)),
                   pl.BlockSpec((1, tm, 2 * dm), lambda bi, i: (bi, i, 0)),
                   pl.BlockSpec((1, dm, tm), lambda bi, i: (bi, 0, i)),
                   pl.BlockSpec((1, tm, ng), lambda bi, i: (bi, i, 0)),
                   pl.BlockSpec((1, ng, tm), lambda bi, i: (bi, 0, i))),
        compiler_params=_params(("arbitrary", "arbitrary")),
        name="inproj",
    )(x, mod_l, gn, wa, wb, wg, gb)


def _mlstm_kernel(qf_ref, qb_ref, vf_ref, vb_ref, pf_ref, pb_ref, ptf_ref, ptb_ref, c0_ref, n0_ref, m0_ref,
                  hf_ref, hb_ref, c1_ref, n1_ref, m1_ref, c_sc, n_sc, m_sc, *, chunks):
    j = pl.program_id(1)
    nc = pl.num_programs(1)
    L = CHUNK
    dm = N_HEADS * HEAD_DIM
    pad = 16

    @pl.when(j == 0)
    def _():
        c_sc[...] = c0_ref[0]
        n_sc[...] = n0_ref[0]
        m_sc[...] = m0_ref[0]

    W = 2 * HEAD_DIM
    row = lax.broadcasted_iota(jnp.int32, (L, W), 0)
    col = lax.broadcasted_iota(jnp.int32, (L, W), 1) % L
    nt_dims = (((1,), (1,)), ((), ()))
    zeros = jnp.zeros((L, HEAD_DIM), BF16)
    pairs = [(d, pr) for d in range(2) for pr in range(N_HEADS // 2)]

    def blockdiag(x):
        return jnp.concatenate([jnp.concatenate([x[:, :HEAD_DIM], zeros], axis=1),
                                jnp.concatenate([zeros, x[:, HEAD_DIM:]], axis=1)], axis=0)

    def per_head(x, lane):
        return jnp.concatenate([jnp.broadcast_to(x[:, lane:lane + 1], (1, HEAD_DIM)),
                                jnp.broadcast_to(x[:, HEAD_DIM + lane:HEAD_DIM + lane + 1], (1, HEAD_DIM))], axis=1)

    def chunk_step(sub_f, sub_b):
        tok = (slice(sub_f * L, (sub_f + 1) * L), slice(sub_b * L, (sub_b + 1) * L))
        ks, vts, cts, nvs, r1 = [], [], [], [], []
        for d, pr in pairs:
            idx = d * (N_HEADS // 2) + pr
            qk_ref = (qf_ref, qb_ref)[d]
            q = qk_ref[0, tok[d], pr * W:(pr + 1) * W]
            k = qk_ref[0, tok[d], dm + pr * W:dm + (pr + 1) * W]
            ct, nvec = c_sc[idx], n_sc[idx]
            lhs = jnp.concatenate([k, ct.astype(BF16), jnp.broadcast_to(nvec, (pad, W)).astype(BF16)], axis=0)
            r1.append(lax.dot_general(lhs, blockdiag(q), nt_dims, preferred_element_type=F32))
            v_ref = (vf_ref, vb_ref)[d]
            vts.append(jnp.concatenate([v_ref[0, (2 * pr) * HEAD_DIM:(2 * pr + 1) * HEAD_DIM, tok[d]],
                                        v_ref[0, (2 * pr + 1) * HEAD_DIM:(2 * pr + 2) * HEAD_DIM, tok[d]]], axis=1))
            ks.append(k), cts.append(ct), nvs.append(nvec)

        w_intra, w_inter, floor, w_state, m_news, r3 = [], [], [], [], [], []
        for c, (d, pr) in enumerate(pairs):
            idx = d * (N_HEADS // 2) + pr
            ci = 2 * d * N_HEADS + 2 * pr
            p = (pf_ref, pb_ref)[d][0, tok[d], :]
            pt = (ptf_ref, ptb_ref)[d][0, :, tok[d]]
            g_col = jnp.concatenate([jnp.broadcast_to(p[:, ci:ci + 1], (L, HEAD_DIM)),
                                     jnp.broadcast_to(p[:, ci + 1:ci + 2], (L, HEAD_DIM))], axis=1)
            i_row = jnp.concatenate([pt[ci:ci + 1, :], pt[ci + 1:ci + 2, :]], axis=1)
            b_row = jnp.concatenate([pt[ci + N_HEADS:ci + N_HEADS + 1, :],
                                     pt[ci + N_HEADS + 1:ci + N_HEADS + 2, :]], axis=1)
            m_prev = m_sc[idx]
            mask = (row <= col) if d == 0 else (row >= col)
            last = L - 1 if d == 0 else 0
            dmat = jnp.where(mask, b_row + g_col, -jnp.inf)
            a = b_row + m_prev
            m_t = jnp.maximum(a, jnp.max(dmat, axis=0, keepdims=True))
            w_intra.append(jnp.exp(dmat - m_t))
            w_inter.append(jnp.exp(a - m_t))
            floor.append(jnp.exp(-m_t))
            m_new = per_head(m_t, last)
            b_last = per_head(b_row, last)
            w_state.append(jnp.exp(b_last + m_prev - m_new))
            m_news.append(m_new)
            w_tok = jnp.exp(b_last - b_row + i_row - m_new)
            lhs = jnp.concatenate([(vts[c].astype(F32) * w_tok).astype(BF16),
                                   jnp.broadcast_to(w_tok, (pad, W)).astype(BF16)], axis=0)
            r3.append(jnp.dot(lhs, blockdiag(ks[c]), preferred_element_type=F32))

        for c, (d, pr) in enumerate(pairs):
            s = r1[c][0:L] * w_intra[c]
            den = w_inter[c] * r1[c][2 * L:2 * L + 1] + jnp.sum(s, axis=0, keepdims=True)
            inv = 1.0 / jnp.maximum(jnp.abs(den), floor[c])
            ht = (jnp.dot(vts[c], blockdiag(s.astype(BF16)), preferred_element_type=F32)
                  + r1[c][L:2 * L] * w_inter[c]) * inv
            out_ref = (hf_ref, hb_ref)[d]
            out_ref[0, (2 * pr) * HEAD_DIM:(2 * pr + 1) * HEAD_DIM, tok[d]] = ht[:, :HEAD_DIM]
            out_ref[0, (2 * pr + 1) * HEAD_DIM:(2 * pr + 2) * HEAD_DIM, tok[d]] = ht[:, HEAD_DIM:]

        for c, (d, pr) in enumerate(pairs):
            idx = d * (N_HEADS // 2) + pr
            c_sc[idx] = w_state[c] * cts[c] + r3[c][0:HEAD_DIM]
            n_sc[idx] = w_state[c] * nvs[c] + r3[c][HEAD_DIM:HEAD_DIM + 1]
            m_sc[idx] = m_news[c]

    for u in range(chunks):
        chunk_step(u, chunks - 1 - u)

    @pl.when(j == nc - 1)
    def _():
        c1_ref[0] = c_sc[...]
        n1_ref[0] = n_sc[...]
        m1_ref[0] = m_sc[...]


def _mlstm_zero_state(bsz):
    npair = N_HEADS
    return (jnp.zeros((bsz, npair, HEAD_DIM, 2 * HEAD_DIM), F32),
            jnp.zeros((bsz, npair, 1, 2 * HEAD_DIM), F32),
            jnp.zeros((bsz, npair, 1, 2 * HEAD_DIM), F32))


def _mlstm(qk, vt, p, pt, state):
    b, n, w = qk.shape
    chunks = min(MLSTM_CHUNKS_PER_STEP, n // CHUNK)
    span = CHUNK * chunks
    nc = n // span
    ng = 4 * N_HEADS
    dm = N_HEADS * HEAD_DIM
    c0, n0, m0 = state
    fwd = lambda bi, j: (bi, j, 0)
    bwd = lambda bi, j: (bi, nc - 1 - j, 0)
    fwd_t = lambda bi, j: (bi, 0, j)
    bwd_t = lambda bi, j: (bi, 0, nc - 1 - j)
    st4 = lambda bi, j: (bi, 0, 0, 0)
    outs = pl.pallas_call(
        functools.partial(_mlstm_kernel, chunks=chunks),
        out_shape=(jax.ShapeDtypeStruct((b, dm, n), F32),
                   jax.ShapeDtypeStruct((b, dm, n), F32),
                   jax.ShapeDtypeStruct(c0.shape, F32),
                   jax.ShapeDtypeStruct(n0.shape, F32),
                   jax.ShapeDtypeStruct(m0.shape, F32)),
        grid=(b, nc),
        in_specs=[pl.BlockSpec((1, span, w), fwd),
                  pl.BlockSpec((1, span, w), bwd),
                  pl.BlockSpec((1, dm, span), fwd_t),
                  pl.BlockSpec((1, dm, span), bwd_t),
                  pl.BlockSpec((1, span, ng), fwd),
                  pl.BlockSpec((1, span, ng), bwd),
                  pl.BlockSpec((1, ng, span), fwd_t),
                  pl.BlockSpec((1, ng, span), bwd_t),
                  pl.BlockSpec((1,) + c0.shape[1:], st4),
                  pl.BlockSpec((1,) + n0.shape[1:], st4),
                  pl.BlockSpec((1,) + m0.shape[1:], st4)],
        out_specs=(pl.BlockSpec((1, dm, span), fwd_t),
                   pl.BlockSpec((1, dm, span), bwd_t),
                   pl.BlockSpec((1,) + c0.shape[1:], st4),
                   pl.BlockSpec((1,) + n0.shape[1:], st4),
                   pl.BlockSpec((1,) + m0.shape[1:], st4)),
        scratch_shapes=[pltpu.VMEM(c0.shape[1:], F32),
                        pltpu.VMEM(n0.shape[1:], F32),
                        pltpu.VMEM(m0.shape[1:], F32)],
        compiler_params=_params(("arbitrary", "arbitrary")),
        name="mlstm",
    )(qk, qk, vt, vt, p, p, pt, pt, c0, n0, m0)
    return outs[0], outs[1], (outs[2], outs[3], outs[4])


def _mixout_kernel(convo_ref, hf_ref, hb_ref, x_ref, mod_ref, cw_ref, cb_ref, gh_ref, wo_ref, gn2_ref, wr_ref,
                   xo_ref, h2_ref, lt_ref, *, tm, row_len):
    dc = cw_ref.shape[1]
    cw = cw_ref[...]
    m = mod_ref[0]
    gain2 = gn2_ref[...] * (1.0 + m[4:5])
    sub = LANES if LANES % row_len == 0 else tm
    t = lax.broadcasted_iota(jnp.int32, (sub, dc), 0) % row_len
    for r in range(tm // sub):
        rows = slice(r * sub, (r + 1) * sub)
        cv = convo_ref[0, rows, :]
        bg = cv[:, 0:dc].astype(F32)
        cg = cv[:, dc:2 * dc].astype(F32)
        xi = cv[:, 2 * dc:3 * dc].astype(F32)
        og = cv[:, 3 * dc:].astype(F32)
        u = cg * xi
        u_prev = jnp.where(t == 0, 0.0, pltpu.roll(u, 1, axis=0))
        u_next = jnp.where(t == row_len - 1, 0.0, pltpu.roll(u, sub - 1, axis=0))
        yc = bg * (cw[0:1] * u_prev + cw[1:2] * u + cw[2:3] * u_next + cb_ref[...])

        hm = hf_ref[0, :, rows] + hb_ref[0, :, rows]
        parts = []
        for hd in range(N_HEADS):
            hh = hm[hd * HEAD_DIM:(hd + 1) * HEAD_DIM, :]
            parts.append(hh * lax.rsqrt(jnp.mean(hh * hh, axis=0, keepdims=True) + EPS))
        ym = _sigmoid(og) * (jnp.concatenate(parts, axis=0).T * gh_ref[...])

        cat = jnp.concatenate([yc, ym], axis=-1).astype(BF16)
        y = jnp.dot(cat, wo_ref[...], preferred_element_type=F32)
        xn = x_ref[0, rows, :] + m[2:3] * y
        xo_ref[0, rows, :] = xn
        ms = jnp.mean(xn * xn, axis=-1, keepdims=True)
        h2 = xn * lax.rsqrt(ms + EPS) * gain2 + m[3:4]
        hi = h2.astype(BF16)
        h2_ref[0, rows, :] = hi
        lo = (h2 - hi.astype(F32)).astype(BF16)
        lg = jnp.dot(jnp.concatenate([hi, hi, lo], axis=1), wr_ref[...], preferred_element_type=F32)
        for q in range(sub // LANES):
            lanes = slice(r * sub + q * LANES, r * sub + (q + 1) * LANES)
            lt_ref[0, :, lanes] = lg[q * LANES:(q + 1) * LANES, :].T[:N_EXPERTS, :]


def _mixout(convo, hf, hb, x, mod_l, mod_row, cw, cb, gh, wo, gn2, wr, tm, row_len):
    b, n, d = x.shape
    per_sample = mod_row is None
    mod_map = (lambda bi, i: (bi, 0, 0)) if per_sample else (lambda bi, i: (mod_row, 0, 0))
    tok = lambda bi, i: (bi, i, 0)
    cst = lambda bi, i: (0, 0)
    return pl.pallas_call(
        functools.partial(_mixout_kernel, tm=tm, row_len=row_len),
        out_shape=(jax.ShapeDtypeStruct((b, n, d), F32),
                   jax.ShapeDtypeStruct((b, n, d), BF16),
                   jax.ShapeDtypeStruct((b, N_EXPERTS, n), F32)),
        grid=(b, n // tm),
        in_specs=[pl.BlockSpec((1, tm, convo.shape[2]), tok),
                  pl.BlockSpec((1, hf.shape[1], tm), lambda bi, i: (bi, 0, i)),
                  pl.BlockSpec((1, hb.shape[1], tm), lambda bi, i: (bi, 0, i)),
                  pl.BlockSpec((1, tm, d), tok),
                  pl.BlockSpec((1, N_MOD, d), mod_map),
                  pl.BlockSpec(cw.shape, cst),
                  pl.BlockSpec(cb.shape, cst),
                  pl.BlockSpec(gh.shape, cst),
                  pl.BlockSpec(wo.shape, cst),
                  pl.BlockSpec(gn2.shape, cst),
                  pl.BlockSpec(wr.shape, cst)],
        out_specs=(pl.BlockSpec((1, tm, d), tok),
                   pl.BlockSpec((1, tm, d), tok),
                   pl.BlockSpec((1, N_EXPERTS, tm), lambda bi, i: (bi, 0, i))),
        compiler_params=_params(("arbitrary", "arbitrary")),
        name="mixout",
    )(convo, hf, hb, x, mod_l, cw, cb, gh, wo, gn2, wr)


def _route_kernel(lt_ref, pos_ref, g_ref, cs_ref, aff_sc, sel_sc, *, n, cap):
    lg = lt_ref[0]
    e = jnp.exp(lg - jnp.max(lg, axis=0, keepdims=True))
    aff_sc[...] = e / jnp.sum(e, axis=0, keepdims=True)
    capf = jnp.float32(cap)

    def count_ge(cand):
        acc = jnp.zeros((N_EXPERTS, LANES), F32)
        for j in range(n // LANES):
            acc = acc + jnp.where(aff_sc[:, j * LANES:(j + 1) * LANES] >= cand, 1.0, 0.0)
        return jnp.sum(acc, axis=1, keepdims=True)

    def exp_step(_, c):
        elo, ehi = c
        emid = jnp.floor((elo + ehi) * 0.5)
        ok = count_ge(jnp.exp2(emid)) >= capf
        return jnp.where(ok, emid, elo), jnp.where(ok, ehi, emid)

    elo, ehi = lax.fori_loop(0, 8, exp_step, (jnp.full((N_EXPERTS, 1), -128.0, F32), jnp.ones((N_EXPERTS, 1), F32)))

    def val_step(_, c):
        lo, hi = c
        mid = lo + (hi - lo) * 0.5
        ok = count_ge(mid) >= capf
        return jnp.where(ok, mid, lo), jnp.where(ok, hi, mid)

    lo, hi = lax.fori_loop(0, 40, val_step, (jnp.where(elo <= -128.0, 0.0, jnp.exp2(elo)), jnp.exp2(ehi)))
    need = capf - count_ge(hi)

    row = lax.broadcasted_iota(jnp.int32, (LANES, LANES), 0)
    col = lax.broadcasted_iota(jnp.int32, (LANES, LANES), 1)
    upper = jnp.where(row < col, 1.0, 0.0).astype(BF16)
    lane = lax.broadcasted_iota(jnp.int32, (N_EXPERTS, LANES), 1)
    per_tile = ROUTE_TILE // LANES

    ceq = jnp.zeros((N_EXPERTS, 1), F32)
    for c in range(n // LANES):
        a = aff_sc[:, c * LANES:(c + 1) * LANES]
        tie = (a >= lo) & (a < hi)
        eq = jnp.where(tie, 1.0, 0.0)
        peq = jnp.dot(eq.astype(BF16), upper, preferred_element_type=F32) + ceq
        sel = (a >= hi) | (tie & (peq < need))
        g_ref[0, :, c * LANES:(c + 1) * LANES] = jnp.where(sel, a, 0.0)
        sel_sc[:, c * LANES:(c + 1) * LANES] = jnp.where(sel, 1.0, 0.0)
        ceq = ceq + jnp.sum(eq, axis=1, keepdims=True)

    csel = jnp.zeros((N_EXPERTS, 1), F32)
    offs = jnp.zeros((N_EXPERTS, LANES), jnp.int32)
    for c in range(n // LANES):
        self = sel_sc[:, c * LANES:(c + 1) * LANES]
        psel = jnp.dot(self.astype(BF16), upper, preferred_element_type=F32) + csel
        pos_ref[0, :, c * LANES:(c + 1) * LANES] = jnp.where(self > 0.0, psel.astype(jnp.int32), -1)
        if c % per_tile == 0:
            offs = jnp.where(lane == c // per_tile, csel.astype(jnp.int32), offs)
        csel = csel + jnp.sum(self, axis=1, keepdims=True)
    cs_ref[0] = jnp.where(lane >= n // ROUTE_TILE, csel.astype(jnp.int32), offs)


def _route(lt):
    b, ne, n = lt.shape
    cap = CAPACITY_FACTOR * n // N_EXPERTS
    blk = pl.BlockSpec((1, ne, n), lambda bi: (bi, 0, 0))
    return pl.pallas_call(
        functools.partial(_route_kernel, n=n, cap=cap),
        out_shape=(jax.ShapeDtypeStruct((b, ne, n), jnp.int32),
                   jax.ShapeDtypeStruct((b, ne, n), F32),
                   jax.ShapeDtypeStruct((b, ne, LANES), jnp.int32)),
        grid=(b,),
        in_specs=[blk],
        out_specs=(blk, blk, pl.BlockSpec((1, ne, LANES), lambda bi: (bi, 0, 0))),
        scratch_shapes=[pltpu.VMEM((ne, n), F32), pltpu.VMEM((ne, n), F32)],
        compiler_params=_params(("arbitrary",)),
        name="route",
    )(lt)


def _ffn_kernel(cs_ref, h2_ref, pos_ref, wg_ref, wu_ref, wd_ref, ye_ref, xe_sc, *, tdma, nt_pad, cap, row_chunk,
                group, early_rows):
    e = pl.program_id(0)
    b = pl.program_id(1)
    s = pl.program_id(2)
    ns = pl.num_programs(2)
    sub = tdma // ROUTE_TILE
    g = b % group

    @pl.when(s == 0)
    def _():
        xe_sc[g] = jnp.zeros(xe_sc.shape[1:], F32)

    cs_base = (b * N_EXPERTS + e) * nt_pad + s * sub

    jrow1 = lax.broadcasted_iota(jnp.int32, (FIRST_CHUNK, ROUTE_TILE), 0)
    starts = [cs_ref[cs_base + i] for i in range(sub + 1)]
    for i in range(sub):
        base = pl.multiple_of(_floor_to(starts[i], 8), 8)
        prow = pos_ref[0, 0, :, i * ROUTE_TILE:(i + 1) * ROUTE_TILE]
        xt = h2_ref[0, i * ROUTE_TILE:(i + 1) * ROUTE_TILE, :]
        onehot = jnp.where(prow == base + jrow1, 1.0, 0.0).astype(BF16)
        xe_sc[g, pl.ds(base, FIRST_CHUNK), :] += jnp.dot(onehot, xt, preferred_element_type=F32)

    jrow = lax.broadcasted_iota(jnp.int32, (SLOT_CHUNK, ROUTE_TILE), 0)

    def tile_body(i, carry):
        first_end = _floor_to(cs_ref[cs_base + i], 8) + FIRST_CHUNK
        c1 = cs_ref[cs_base + i + 1]
        nch = jnp.where(c1 > first_end, _ceil_div(c1 - first_end, SLOT_CHUNK), 0)
        off = pl.multiple_of(i * ROUTE_TILE, ROUTE_TILE)

        def chunk(r, c):
            rb = pl.multiple_of(first_end + r * SLOT_CHUNK, 8)
            prow = pos_ref[0, 0, :, pl.ds(off, ROUTE_TILE)]
            onehot = jnp.where(prow == rb + jrow, 1.0, 0.0).astype(BF16)
            xe_sc[g, pl.ds(rb, SLOT_CHUNK), :] += jnp.dot(onehot, h2_ref[0, pl.ds(off, ROUTE_TILE), :],
                                                          preferred_element_type=F32)
            return c

        return lax.fori_loop(0, nch, chunk, carry)

    overflow = functools.reduce(jnp.maximum, [starts[i + 1] - _floor_to(starts[i], 8) for i in range(sub)])

    @pl.when(overflow > FIRST_CHUNK)
    def _():
        lax.fori_loop(0, sub, tile_body, 0)

    def expert(xb):
        a = jnp.dot(xb, wg_ref[0], preferred_element_type=F32)
        u = jnp.dot(xb, wu_ref[0], preferred_element_type=F32)
        hm = (a * _sigmoid(a) * u).astype(BF16)
        return jnp.dot(hm, wd_ref[0], preferred_element_type=F32).astype(BF16)

    def run_rows(lo, hi):
        ye_ref[0, 0, lo:hi, :] = expert(xe_sc[0, lo:hi, :].astype(BF16))

    filled_first = cs_ref[(b * N_EXPERTS + e) * nt_pad + sub]

    if early_rows:
        @pl.when((s == 0) & (filled_first >= early_rows))
        def _():
            run_rows(0, early_rows)

    @pl.when((s == ns - 1) & (g == group - 1))
    def _():
        if group == 1:
            if early_rows:
                @pl.when(filled_first < early_rows)
                def _():
                    run_rows(0, early_rows)
            for rc in range((cap - early_rows) // row_chunk):
                run_rows(early_rows + rc * row_chunk, early_rows + (rc + 1) * row_chunk)
        else:
            y = expert(jnp.concatenate([xe_sc[i, 0:cap, :] for i in range(group)], axis=0).astype(BF16))
            for i in range(group):
                ye_ref[i, 0, 0:cap, :] = y[i * cap:(i + 1) * cap]
        for i in range(group):
            ye_ref[i, 0, cap:, :] = jnp.zeros((ye_ref.shape[2] - cap, ye_ref.shape[3]), BF16)


def _ffn(cs_flat, nt_pad, h2, pos4, wg, wu, wd):
    b, n, d = h2.shape
    ne, _, f = wg.shape
    cap = CAPACITY_FACTOR * n // N_EXPERTS
    rows = cap + SLOT_PAD
    tdma = min(n, 4096)
    row_chunk = min(cap, FFN_ROWS)
    group = b if b * cap <= FFN_ROWS else 1
    early_rows = 0
    if group == 1 and n // tdma == 2 and cap > 2 * FFN_EARLY_ROWS and (cap - FFN_EARLY_ROWS) % 32 == 0:
        early_rows = FFN_EARLY_ROWS
        row_chunk = (cap - early_rows) // 2
    grid_spec = pltpu.PrefetchScalarGridSpec(
        num_scalar_prefetch=1,
        grid=(ne, b, n // tdma),
        in_specs=[pl.BlockSpec((1, tdma, d), lambda e, bi, s, cs: (bi, s, 0)),
                  pl.BlockSpec((1, 1, 1, tdma), lambda e, bi, s, cs: (bi, e, 0, s)),
                  pl.BlockSpec((1, d, f), lambda e, bi, s, cs: (e, 0, 0)),
                  pl.BlockSpec((1, d, f), lambda e, bi, s, cs: (e, 0, 0)),
                  pl.BlockSpec((1, f, d), lambda e, bi, s, cs: (e, 0, 0))],
        out_specs=pl.BlockSpec((group, 1, rows, d), lambda e, bi, s, cs: (bi // group, e, 0, 0)),
        scratch_shapes=[pltpu.VMEM((group, rows, d), F32)])
    return pl.pallas_call(
        functools.partial(_ffn_kernel, tdma=tdma, nt_pad=nt_pad, cap=cap, row_chunk=row_chunk, group=group,
                          early_rows=early_rows),
        out_shape=jax.ShapeDtypeStruct((b, ne, rows, d), BF16),
        grid_spec=grid_spec,
        compiler_params=_params(("arbitrary", "arbitrary", "arbitrary")),
        name="ffn",
    )(cs_flat, h2, pos4, wg, wu, wd)


def _combine_kernel(cs_ref, pos_ref, g_ref, x_ref, mod_ref, gfin_ref, ye_hbm, out_ref,
                    ystage, pt_sc, acc_sc, sem, *, nt_pad, final):
    b = pl.program_id(0)
    i = pl.program_id(1)
    nt = pl.num_programs(1)
    step = b * nt + i
    slot = step % 2
    group = 4 * SLOT_CHUNK

    @pl.when(step == 0)
    def _():
        ystage[...] = jnp.zeros_like(ystage)

    def chunk_plan(bb, ii, e):
        c0 = cs_ref[(bb * N_EXPERTS + e) * nt_pad + ii]
        c1 = cs_ref[(bb * N_EXPERTS + e) * nt_pad + ii + 1]
        base = _floor_to(c0, 16)
        return base, jnp.where(c1 > c0, _ceil_div(c1 - base, SLOT_CHUNK), 0)

    def chunk_copy(bb, e, rb, sl, k):
        return pltpu.make_async_copy(ye_hbm.at[bb, e, pl.ds(rb, SLOT_CHUNK), :],
                                     ystage.at[sl, pl.ds(k * SLOT_CHUNK, SLOT_CHUNK), :], sem.at[sl])

    def extra_chunks(plans):
        return sum(jnp.maximum(nch - 1, 0) for _, nch in plans)

    def issue(bb, ii, sl):
        plans = [chunk_plan(bb, ii, e) for e in range(N_EXPERTS)]
        for e, (base, _) in enumerate(plans):
            chunk_copy(bb, e, pl.multiple_of(base, 16), sl, e).start()

        @pl.when(extra_chunks(plans) > 0)
        def _():
            k = jnp.int32(N_EXPERTS)
            for e, (base, nch) in enumerate(plans):
                def start(r, kk, e=e, base=base):
                    chunk_copy(bb, e, pl.multiple_of(base + r * SLOT_CHUNK, 16), sl, kk).start()
                    return kk + 1

                k = lax.fori_loop(1, nch, start, k)

    @pl.when(step == 0)
    def _():
        issue(b, i, slot)

    @pl.when(step + 1 < pl.num_programs(0) * nt)
    def _():
        wrap = i + 1 == nt
        issue(jnp.where(wrap, b + 1, b), jnp.where(wrap, 0, i + 1), 1 - slot)

    jrow = lax.broadcasted_iota(jnp.int32, (SLOT_CHUNK, ROUTE_TILE), 0)
    plans = [chunk_plan(b, i, e) for e in range(N_EXPERTS)]
    for e, (base, _) in enumerate(plans):
        pt_sc[e * SLOT_CHUNK:(e + 1) * SLOT_CHUNK, :] = jnp.where(
            pos_ref[0, e:e + 1, :] == base + jrow, g_ref[0, e:e + 1, :], 0.0)
    k = N_EXPERTS + extra_chunks(plans)

    @pl.when(k > N_EXPERTS)
    def _():
        kk0 = jnp.int32(N_EXPERTS)
        for e, (base, nch) in enumerate(plans):
            def weights(r, kk, e=e, base=base):
                row0 = pl.multiple_of(kk * SLOT_CHUNK, SLOT_CHUNK)
                pt_sc[pl.ds(row0, SLOT_CHUNK), :] = jnp.where(
                    pos_ref[0, e:e + 1, :] == base + r * SLOT_CHUNK + jrow, g_ref[0, e:e + 1, :], 0.0)
                return kk + 1

            kk0 = lax.fori_loop(1, nch, weights, kk0)

    kpad = jnp.maximum(_ceil_div(k, 4) * 4, COMBINE_STATIC_CHUNKS)

    def zero_pad(kk, c):
        row0 = pl.multiple_of(kk * SLOT_CHUNK, SLOT_CHUNK)
        pt_sc[pl.ds(row0, SLOT_CHUNK), :] = jnp.zeros((SLOT_CHUNK, ROUTE_TILE), F32)
        return c

    lax.fori_loop(k, kpad, zero_pad, 0)

    def wait_one(kk, c):
        chunk_copy(0, 0, 0, slot, kk).wait()
        return c

    lax.fori_loop(0, k, wait_one, 0)

    static_rows = COMBINE_STATIC_CHUNKS * SLOT_CHUNK
    w = pt_sc[0:static_rows, :].T.astype(BF16)
    acc_sc[...] = jnp.dot(w, ystage[slot, 0:static_rows, :], preferred_element_type=F32)

    def matmul_group(gi, c):
        row0 = pl.multiple_of(gi * group, group)
        wg = pt_sc[pl.ds(row0, group), :].T.astype(BF16)
        acc_sc[...] += jnp.dot(wg, ystage[slot, pl.ds(row0, group), :], preferred_element_type=F32)
        return c

    lax.fori_loop(COMBINE_STATIC_CHUNKS // 4, kpad // 4, matmul_group, 0)

    xn = x_ref[0] + mod_ref[0][5:6] * acc_sc[...]
    if final:
        ms = jnp.mean(xn * xn, axis=-1, keepdims=True)
        xn = xn * lax.rsqrt(ms + EPS) * gfin_ref[...]
    out_ref[0] = xn


def _combine(cs_flat, nt_pad, pos, g, x, mod_l, mod_row, gfin, ye, final):
    b, n, d = x.shape
    per_sample = mod_row is None
    mod_map = ((lambda bi, i, cs: (bi, 0, 0)) if per_sample else (lambda bi, i, cs: (mod_row, 0, 0)))
    max_chunks = N_EXPERTS * ((ROUTE_TILE + 15 + SLOT_CHUNK - 1) // SLOT_CHUNK) + 4
    grid_spec = pltpu.PrefetchScalarGridSpec(
        num_scalar_prefetch=1,
        grid=(b, n // ROUTE_TILE),
        in_specs=[pl.BlockSpec((1, N_EXPERTS, ROUTE_TILE), lambda bi, i, cs: (bi, 0, i)),
                  pl.BlockSpec((1, N_EXPERTS, ROUTE_TILE), lambda bi, i, cs: (bi, 0, i)),
                  pl.BlockSpec((1, ROUTE_TILE, d), lambda bi, i, cs: (bi, i, 0)),
                  pl.BlockSpec((1, N_MOD, d), mod_map),
                  pl.BlockSpec((1, d), lambda bi, i, cs: (0, 0)),
                  pl.BlockSpec(memory_space=pl.ANY)],
        out_specs=pl.BlockSpec((1, ROUTE_TILE, d), lambda bi, i, cs: (bi, i, 0)),
        scratch_shapes=[pltpu.VMEM((2, max_chunks * SLOT_CHUNK, d), BF16),
                        pltpu.VMEM((max_chunks * SLOT_CHUNK, ROUTE_TILE), F32),
                        pltpu.VMEM((ROUTE_TILE, d), F32),
                        pltpu.SemaphoreType.DMA((2,))])
    return pl.pallas_call(
        functools.partial(_combine_kernel, nt_pad=nt_pad, final=final),
        out_shape=jax.ShapeDtypeStruct((b, n, d), F32),
        grid_spec=grid_spec,
        compiler_params=_params(("arbitrary", "arbitrary")),
        name="combine",
    )(cs_flat, pos, g, x, mod_l, gfin, ye)


def _moe(lt, h2, x, mod_l, mod_row, gfin, wg, wu, wd, final):
    b, n, _ = x.shape
    pos, g, cs = _route(lt)
    nt_pad = ((n // ROUTE_TILE + 1 + 7) // 8) * 8
    cs_flat = cs[:, :, :nt_pad].reshape(-1)
    ye = _ffn(cs_flat, nt_pad, h2, pos.reshape(b, N_EXPERTS, 1, n), wg, wu, wd)
    return _combine(cs_flat, nt_pad, pos, g, x, mod_l, mod_row, gfin, ye, final)


def kernel(x, c, ctx, c_ctx, w_mod, b_mod, g_norm1, g_norm2, w_in, w_out, conv_w, conv_b, gate_b, g_head,
           w_router, w_gate_e, w_up_e, w_down_e, g_final):
    bsz, n, d = x.shape
    nctx = ctx.shape[1]
    depth = w_mod.shape[0]
    dm = N_HEADS * HEAD_DIM
    dc = conv_w.shape[2]
    ng = 4 * N_HEADS
    ctx_row = bsz

    rows = ((bsz + 1 + 7) // 8) * 8
    cc = jnp.zeros((rows, d), F32).at[:bsz].set(c).at[bsz].set(c_ctx)
    mod = _modulation(cc, w_mod, b_mod).reshape(depth, rows, N_MOD, d)
    gfin = g_final.reshape(1, d)

    xl, xc = x, ctx
    for layer in range(depth):
        last = layer == depth - 1
        mod_l = mod[layer]
        wi = w_in[layer]
        wa = jnp.concatenate([wi[:, :3 * dc], wi[:, 3 * dc + 3 * dm:3 * dc + 4 * dm]], axis=1).astype(BF16)
        wb = wi[:, 3 * dc:3 * dc + 3 * dm].astype(BF16)
        wgt = jnp.zeros((d, LANES), BF16).at[:, :ng].set(wi[:, 3 * dc + 4 * dm:].astype(BF16))
        gb = jnp.zeros((1, LANES), F32).at[0, :ng].set(gate_b[layer].reshape(-1))
        gn1 = g_norm1[layer].reshape(1, d)
        gn2 = g_norm2[layer].reshape(1, d)
        cw = jnp.zeros((8, dc), F32).at[:conv_w.shape[1]].set(conv_w[layer])
        cb = conv_b[layer].reshape(1, dc)
        gh = g_head[layer].reshape(1, dm)
        wo = w_out[layer].astype(BF16)
        wr32 = jnp.zeros((d, LANES), F32).at[:, :N_EXPERTS].set(w_router[layer])
        wr_hi = wr32.astype(BF16)
        wr = jnp.concatenate([wr_hi, (wr32 - wr_hi.astype(F32)).astype(BF16), wr_hi], axis=0)
        wge = w_gate_e[layer].astype(BF16)
        wue = w_up_e[layer].astype(BF16)
        wde = w_down_e[layer].astype(BF16)

        convo_c, qk_c, vt_c, p_c, pt_c = _inproj(xc, mod_l, ctx_row, gn1, wa, wb, wgt, gb, tm=nctx)
        convo_l, qk_l, vt_l, p_l, pt_l = _inproj(xl, mod_l, None, gn1, wa, wb, wgt, gb, tm=512)
        zero_state = _mlstm_zero_state(bsz)
        hf_c, hb_c, state = _mlstm(qk_c, vt_c, p_c, pt_c, zero_state)
        hf_l, hb_l, _ = _mlstm(qk_l, vt_l, p_l, pt_l, state)

        xl, h2_l, lt_l = _mixout(convo_l, hf_l, hb_l, xl, mod_l, None, cw, cb, gh, wo, gn2, wr,
                                 tm=512, row_len=GRID_W)
        xl = _moe(lt_l, h2_l, xl, mod_l, None, gfin, wge, wue, wde, final=last)
        if not last:
            xc, h2_c, lt_c = _mixout(convo_c, hf_c, hb_c, xc, mod_l, ctx_row, cw, cb, gh, wo, gn2, wr,
                                     tm=nctx, row_len=nctx)
            xc = _moe(lt_c, h2_c, xc, mod_l, ctx_row, gfin, wge, wue, wde, final=False)
    return xl
```

```python
import functools

import jax
import jax.numpy as jnp
from jax import lax
from jax.experimental import pallas as pl
from jax.experimental.pallas import tpu as pltpu

EPS = 1e-6
N_HEADS = 4
HEAD_DIM = 128
CHUNK = 128
MLSTM_CHUNKS_PER_STEP = 8
INPROJ_TILE = 512
MIXOUT_TILE = 1024
N_EXPERTS = 16
CAPACITY_FACTOR = 2
N_MOD = 6
GRID_W = 64
LANES = 128
ROUTE_TILE = 256
SLOT_CHUNK = 64
FIRST_CHUNK = 64
FFN_ROWS = 512
FFN_EARLY_ROWS = 384
SLOT_PAD = 128
COMBINE_STATIC_CHUNKS = 20
VMEM_LIMIT = 56 * 1024 * 1024

F32 = jnp.float32
BF16 = jnp.bfloat16
HIGHEST = lax.Precision.HIGHEST


def _params(sem, vmem=VMEM_LIMIT):
    return pltpu.CompilerParams(dimension_semantics=sem, vmem_limit_bytes=vmem)


def _sigmoid(x):
    return 1.0 / (1.0 + jnp.exp(-x))


def _floor_to(x, m):
    return jnp.bitwise_and(x, -m)


def _ceil_div(x, m):
    return jnp.right_shift(x + (m - 1), m.bit_length() - 1)


def _mod_kernel(c_ref, w_ref, b_ref, o_ref):
    c = c_ref[...]
    s = c * _sigmoid(c)
    o_ref[0] = jnp.dot(s, w_ref[0], precision=HIGHEST, preferred_element_type=F32) + b_ref[0]


def _modulation(cc, w_mod, b_mod):
    depth, d, nm = w_mod.shape
    rows = cc.shape[0]
    tn = 1536
    return pl.pallas_call(
        _mod_kernel,
        out_shape=jax.ShapeDtypeStruct((depth, rows, nm), F32),
        grid=(depth, nm // tn),
        in_specs=[pl.BlockSpec((rows, d), lambda l, j: (0, 0)),
                  pl.BlockSpec((1, d, tn), lambda l, j: (l, 0, j)),
                  pl.BlockSpec((1, 1, tn), lambda l, j: (l, 0, j))],
        out_specs=pl.BlockSpec((1, rows, tn), lambda l, j: (l, 0, j)),
        compiler_params=_params(("arbitrary", "arbitrary")),
        name="modulation",
    )(cc, w_mod, b_mod.reshape(depth, 1, nm))


def _inproj_kernel(x_ref, mod_ref, gn_ref, wa_ref, wb_ref, wg_ref, gb_ref,
                   convo_ref, qk_ref, vt_ref, p_ref, pt_ref, *, tm):
    x = x_ref[0]
    m = mod_ref[0]
    ms = jnp.mean(x * x, axis=-1, keepdims=True)
    h = x * lax.rsqrt(ms + EPS) * (gn_ref[...] * (1.0 + m[1:2])) + m[0:1]
    hb = h.astype(BF16)
    cw = 512
    for c in range(wa_ref.shape[1] // cw):
        convo_ref[0, :, c * cw:(c + 1) * cw] = jnp.dot(
            hb, wa_ref[:, c * cw:(c + 1) * cw], preferred_element_type=F32).astype(BF16)
    q = jnp.dot(hb, wb_ref[:, 0:cw], preferred_element_type=F32) * (HEAD_DIM ** -0.5)
    qk_ref[0, :, 0:cw] = q.astype(BF16)
    qk_ref[0, :, cw:2 * cw] = jnp.dot(hb, wb_ref[:, cw:2 * cw], preferred_element_type=F32).astype(BF16)
    v = jnp.dot(hb, wb_ref[:, 2 * cw:3 * cw], preferred_element_type=F32)
    vt_ref[0] = v.T.astype(BF16)

    g = jnp.dot(hb, wg_ref[...], preferred_element_type=F32) + gb_ref[...]
    kind = lax.broadcasted_iota(jnp.int32, (1, LANES), 1) // N_HEADS
    logf = jnp.minimum(g, 0.0) - jnp.log1p(jnp.exp(-jnp.abs(g)))
    g = jnp.where((kind == 1) | (kind == 3), logf, g)
    row = lax.broadcasted_iota(jnp.int32, (CHUNK, CHUNK), 0)
    col = lax.broadcasted_iota(jnp.int32, (CHUNK, CHUNK), 1)
    tri = jnp.concatenate([jnp.where(col <= row, 1.0, 0.0), jnp.where(col >= row, 1.0, 0.0)], axis=0).astype(BF16)
    for r in range(tm // CHUNK):
        gc = g[r * CHUNK:(r + 1) * CHUNK]
        g_hi = gc.astype(BF16)
        rem = gc - g_hi.astype(F32)
        g_mid = rem.astype(BF16)
        g_lo = (rem - g_mid.astype(F32)).astype(BF16)
        sums = jnp.dot(tri, jnp.concatenate([g_hi, g_mid, g_lo], axis=1), preferred_element_type=F32)
        sums = sums[:, 0:LANES] + sums[:, LANES:2 * LANES] + sums[:, 2 * LANES:3 * LANES]
        cum, suf = sums[0:CHUNK], sums[CHUNK:2 * CHUNK]
        p = jnp.where(kind == 1, cum, jnp.where(kind == 3, suf, gc))
        pt_ref[0, :, r * CHUNK:(r + 1) * CHUNK] = p.T[:4 * N_HEADS, :]
        diff = p - pltpu.roll(p, LANES - N_HEADS, axis=1)
        p_ref[0, r * CHUNK:(r + 1) * CHUNK, :] = diff[:, :4 * N_HEADS]


def _inproj(x, mod_l, mod_row, gn, wa, wb, wg, gb, tm):
    b, n, d = x.shape
    per_sample = mod_row is None
    mod_map = (lambda bi, i: (bi, 0, 0)) if per_sample else (lambda bi, i: (mod_row, 0, 0))
    ng = 4 * N_HEADS
    dm = N_HEADS * HEAD_DIM
    return pl.pallas_call(
        functools.partial(_inproj_kernel, tm=tm),
        out_shape=(jax.ShapeDtypeStruct((b, n, wa.shape[1]), BF16),
                   jax.ShapeDtypeStruct((b, n, 2 * dm), BF16),
                   jax.ShapeDtypeStruct((b, dm, n), BF16),
                   jax.ShapeDtypeStruct((b, n, ng), F32),
                   jax.ShapeDtypeStruct((b, ng, n), F32)),
        grid=(b, n // tm),
        in_specs=[pl.BlockSpec((1, tm, d), lambda bi, i: (bi, i, 0)),
                  pl.BlockSpec((1, N_MOD, d), mod_map),
                  pl.BlockSpec((1, d), lambda bi, i: (0, 0)),
                  pl.BlockSpec(wa.shape, lambda bi, i: (0, 0)),
                  pl.BlockSpec(wb.shape, lambda bi, i: (0, 0)),
                  pl.BlockSpec(wg.shape, lambda bi, i: (0, 0)),
                  pl.BlockSpec((1, LANES), lambda bi, i: (0, 0))],
        out_specs=(pl.BlockSpec((1, tm, wa.shape[1]), lambda bi, i: (bi, i, 0)),
                   pl.BlockSpec((1, tm, 2 * dm), lambda bi, i: (bi, i, 0)),
                   pl.BlockSpec((1, dm, tm), lambda bi, i: (bi, 0, i)),
                   pl.BlockSpec((1, tm, ng), lambda bi, i: (bi, i, 0)),
                   pl.BlockSpec((1, ng, tm), lambda bi, i: (bi, 0, i))),
        compiler_params=_params(("arbitrary", "arbitrary")),
        name="inproj",
    )(x, mod_l, gn, wa, wb, wg, gb)


def _mlstm_kernel(qf_ref, qb_ref, vf_ref, vb_ref, pf_ref, pb_ref, ptf_ref, ptb_ref, c0_ref, n0_ref, m0_ref,
                  hf_ref, hb_ref, c1_ref, n1_ref, m1_ref, c_sc, n_sc, m_sc, *, chunks):
    j = pl.program_id(1)
    nc = pl.num_programs(1)
    L = CHUNK
    dm = N_HEADS * HEAD_DIM
    pad = 16

    @pl.when(j == 0)
    def _():
        c_sc[...] = c0_ref[0]
        n_sc[...] = n0_ref[0]
        m_sc[...] = m0_ref[0]

    W = 2 * HEAD_DIM
    row = lax.broadcasted_iota(jnp.int32, (L, W), 0)
    col = lax.broadcasted_iota(jnp.int32, (L, W), 1) % L
    nt_dims = (((1,), (1,)), ((), ()))
    zeros = jnp.zeros((L, HEAD_DIM), BF16)
    pairs = [(d, pr) for d in range(2) for pr in range(N_HEADS // 2)]

    def blockdiag(x):
        return jnp.concatenate([jnp.concatenate([x[:, :HEAD_DIM], zeros], axis=1),
                                jnp.concatenate([zeros, x[:, HEAD_DIM:]], axis=1)], axis=0)

    def per_head(x, lane):
        return jnp.concatenate([jnp.broadcast_to(x[:, lane:lane + 1], (1, HEAD_DIM)),
                                jnp.broadcast_to(x[:, HEAD_DIM + lane:HEAD_DIM + lane + 1], (1, HEAD_DIM))], axis=1)

    def chunk_step(sub_f, sub_b):
        tok = (slice(sub_f * L, (sub_f + 1) * L), slice(sub_b * L, (sub_b + 1) * L))
        ks, vts, cts, nvs, r1 = [], [], [], [], []
        for d, pr in pairs:
            idx = d * (N_HEADS // 2) + pr
            qk_ref = (qf_ref, qb_ref)[d]
            q = qk_ref[0, tok[d], pr * W:(pr + 1) * W]
            k = qk_ref[0, tok[d], dm + pr * W:dm + (pr + 1) * W]
            ct, nvec = c_sc[idx], n_sc[idx]
            lhs = jnp.concatenate([k, ct.astype(BF16), jnp.broadcast_to(nvec, (pad, W)).astype(BF16)], axis=0)
            r1.append(lax.dot_general(lhs, blockdiag(q), nt_dims, preferred_element_type=F32))
            v_ref = (vf_ref, vb_ref)[d]
            vts.append(jnp.concatenate([v_ref[0, (2 * pr) * HEAD_DIM:(2 * pr + 1) * HEAD_DIM, tok[d]],
                                        v_ref[0, (2 * pr + 1) * HEAD_DIM:(2 * pr + 2) * HEAD_DIM, tok[d]]], axis=1))
            ks.append(k), cts.append(ct), nvs.append(nvec)

        w_intra, w_inter, floor, w_state, m_news, r3 = [], [], [], [], [], []
        for c, (d, pr) in enumerate(pairs):
            idx = d * (N_HEADS // 2) + pr
            ci = 2 * d * N_HEADS + 2 * pr
            p = (pf_ref, pb_ref)[d][0, tok[d], :]
            pt = (ptf_ref, ptb_ref)[d][0, :, tok[d]]
            g_col = jnp.concatenate([jnp.broadcast_to(p[:, ci:ci + 1], (L, HEAD_DIM)),
                                     jnp.broadcast_to(p[:, ci + 1:ci + 2], (L, HEAD_DIM))], axis=1)
            i_row = jnp.concatenate([pt[ci:ci + 1, :], pt[ci + 1:ci + 2, :]], axis=1)
            b_row = jnp.concatenate([pt[ci + N_HEADS:ci + N_HEADS + 1, :],
                                     pt[ci + N_HEADS + 1:ci + N_HEADS + 2, :]], axis=1)
            m_prev = m_sc[idx]
            mask = (row <= col) if d == 0 else (row >= col)
            last = L - 1 if d == 0 else 0
            dmat = jnp.where(mask, b_row + g_col, -jnp.inf)
            a = b_row + m_prev
            m_t = jnp.maximum(a, jnp.max(dmat, axis=0, keepdims=True))
            w_intra.append(jnp.exp(dmat - m_t))
            w_inter.append(jnp.exp(a - m_t))
            floor.append(jnp.exp(-m_t))
            m_new = per_head(m_t, last)
            b_last = per_head(b_row, last)
            w_state.append(jnp.exp(b_last + m_prev - m_new))
            m_news.append(m_new)
            w_tok = jnp.exp(b_last - b_row + i_row - m_new)
            lhs = jnp.concatenate([(vts[c].astype(F32) * w_tok).astype(BF16),
                                   jnp.broadcast_to(w_tok, (pad, W)).astype(BF16)], axis=0)
            r3.append(jnp.dot(lhs, blockdiag(ks[c]), preferred_element_type=F32))

        for c, (d, pr) in enumerate(pairs):
            s = r1[c][0:L] * w_intra[c]
            den = w_inter[c] * r1[c][2 * L:2 * L + 1] + jnp.sum(s, axis=0, keepdims=True)
            inv = 1.0 / jnp.maximum(jnp.abs(den), floor[c])
            ht = (jnp.dot(vts[c], blockdiag(s.astype(BF16)), preferred_element_type=F32)
                  + r1[c][L:2 * L] * w_inter[c]) * inv
            out_ref = (hf_ref, hb_ref)[d]
            out_ref[0, (2 * pr) * HEAD_DIM:(2 * pr + 1) * HEAD_DIM, tok[d]] = ht[:, :HEAD_DIM]
            out_ref[0, (2 * pr + 1) * HEAD_DIM:(2 * pr + 2) * HEAD_DIM, tok[d]] = ht[:, HEAD_DIM:]

        for c, (d, pr) in enumerate(pairs):
            idx = d * (N_HEADS // 2) + pr
            c_sc[idx] = w_state[c] * cts[c] + r3[c][0:HEAD_DIM]
            n_sc[idx] = w_state[c] * nvs[c] + r3[c][HEAD_DIM:HEAD_DIM + 1]
            m_sc[idx] = m_news[c]

    for u in range(chunks):
        chunk_step(u, chunks - 1 - u)

    @pl.when(j == nc - 1)
    def _():
        c1_ref[0] = c_sc[...]
        n1_ref[0] = n_sc[...]
        m1_ref[0] = m_sc[...]


def _mlstm_zero_state(bsz):
    npair = N_HEADS
    return (jnp.zeros((bsz, npair, HEAD_DIM, 2 * HEAD_DIM), F32),
            jnp.zeros((bsz, npair, 1, 2 * HEAD_DIM), F32),
            jnp.zeros((bsz, npair, 1, 2 * HEAD_DIM), F32))


def _mlstm(qk, vt, p, pt, state):
    b, n, w = qk.shape
    chunks = min(MLSTM_CHUNKS_PER_STEP, n // CHUNK)
    span = CHUNK * chunks
    nc = n // span
    ng = 4 * N_HEADS
    dm = N_HEADS * HEAD_DIM
    c0, n0, m0 = state
    fwd = lambda bi, j: (bi, j, 0)
    bwd = lambda bi, j: (bi, nc - 1 - j, 0)
    fwd_t = lambda bi, j: (bi, 0, j)
    bwd_t = lambda bi, j: (bi, 0, nc - 1 - j)
    st4 = lambda bi, j: (bi, 0, 0, 0)
    outs = pl.pallas_call(
        functools.partial(_mlstm_kernel, chunks=chunks),
        out_shape=(jax.ShapeDtypeStruct((b, dm, n), F32),
                   jax.ShapeDtypeStruct((b, dm, n), F32),
                   jax.ShapeDtypeStruct(c0.shape, F32),
                   jax.ShapeDtypeStruct(n0.shape, F32),
                   jax.ShapeDtypeStruct(m0.shape, F32)),
        grid=(b, nc),
        in_specs=[pl.BlockSpec((1, span, w), fwd),
                  pl.BlockSpec((1, span, w), bwd),
                  pl.BlockSpec((1, dm, span), fwd_t),
                  pl.BlockSpec((1, dm, span), bwd_t),
                  pl.BlockSpec((1, span, ng), fwd),
                  pl.BlockSpec((1, span, ng), bwd),
                  pl.BlockSpec((1, ng, span), fwd_t),
                  pl.BlockSpec((1, ng, span), bwd_t),
                  pl.BlockSpec((1,) + c0.shape[1:], st4),
                  pl.BlockSpec((1,) + n0.shape[1:], st4),
                  pl.BlockSpec((1,) + m0.shape[1:], st4)],
        out_specs=(pl.BlockSpec((1, dm, span), fwd_t),
                   pl.BlockSpec((1, dm, span), bwd_t),
                   pl.BlockSpec((1,) + c0.shape[1:], st4),
                   pl.BlockSpec((1,) + n0.shape[1:], st4),
                   pl.BlockSpec((1,) + m0.shape[1:], st4)),
        scratch_shapes=[pltpu.VMEM(c0.shape[1:], F32),
                        pltpu.VMEM(n0.shape[1:], F32),
                        pltpu.VMEM(m0.shape[1:], F32)],
        compiler_params=_params(("arbitrary", "arbitrary")),
        name="mlstm",
    )(qk, qk, vt, vt, p, p, pt, pt, c0, n0, m0)
    return outs[0], outs[1], (outs[2], outs[3], outs[4])


def _mixout_kernel(convo_ref, hf_ref, hb_ref, x_ref, mod_ref, cw_ref, cb_ref, gh_ref, wo_ref, gn2_ref, wr_ref,
                   xo_ref, h2_ref, lt_ref, *, tm, row_len):
    dc = cw_ref.shape[1]
    cw = cw_ref[...]
    m = mod_ref[0]
    gain2 = gn2_ref[...] * (1.0 + m[4:5])
    sub = LANES if LANES % row_len == 0 else tm
    t = lax.broadcasted_iota(jnp.int32, (sub, dc), 0) % row_len
    for r in range(tm // sub):
        rows = slice(r * sub, (r + 1) * sub)
        cv = convo_ref[0, rows, :]
        bg = cv[:, 0:dc].astype(F32)
        cg = cv[:, dc:2 * dc].astype(F32)
        xi = cv[:, 2 * dc:3 * dc].astype(F32)
        og = cv[:, 3 * dc:].astype(F32)
        u = cg * xi
        u_prev = jnp.where(t == 0, 0.0, pltpu.roll(u, 1, axis=0))
        u_next = jnp.where(t == row_len - 1, 0.0, pltpu.roll(u, sub - 1, axis=0))
        yc = bg * (cw[0:1] * u_prev + cw[1:2] * u + cw[2:3] * u_next + cb_ref[...])

        hm = hf_ref[0, :, rows] + hb_ref[0, :, rows]
        parts = []
        for hd in range(N_HEADS):
            hh = hm[hd * HEAD_DIM:(hd + 1) * HEAD_DIM, :]
            parts.append(hh * lax.rsqrt(jnp.mean(hh * hh, axis=0, keepdims=True) + EPS))
        ym = _sigmoid(og) * (jnp.concatenate(parts, axis=0).T * gh_ref[...])

        cat = jnp.concatenate([yc, ym], axis=-1).astype(BF16)
        y = jnp.dot(cat, wo_ref[...], preferred_element_type=F32)
        xn = x_ref[0, rows, :] + m[2:3] * y
        xo_ref[0, rows, :] = xn
        ms = jnp.mean(xn * xn, axis=-1, keepdims=True)
        h2 = xn * lax.rsqrt(ms + EPS) * gain2 + m[3:4]
        hi = h2.astype(BF16)
        h2_ref[0, rows, :] = hi
        lo = (h2 - hi.astype(F32)).astype(BF16)
        lg = jnp.dot(jnp.concatenate([hi, hi, lo], axis=1), wr_ref[...], preferred_element_type=F32)
        for q in range(sub // LANES):
            lanes = slice(r * sub + q * LANES, r * sub + (q + 1) * LANES)
            lt_ref[0, :, lanes] = lg[q * LANES:(q + 1) * LANES, :].T[:N_EXPERTS, :]


def _mixout(convo, hf, hb, x, mod_l, mod_row, cw, cb, gh, wo, gn2, wr, tm, row_len):
    b, n, d = x.shape
    per_sample = mod_row is None
    mod_map = (lambda bi, i: (bi, 0, 0)) if per_sample else (lambda bi, i: (mod_row, 0, 0))
    tok = lambda bi, i: (bi, i, 0)
    cst = lambda bi, i: (0, 0)
    return pl.pallas_call(
        functools.partial(_mixout_kernel, tm=tm, row_len=row_len),
        out_shape=(jax.ShapeDtypeStruct((b, n, d), F32),
                   jax.ShapeDtypeStruct((b, n, d), BF16),
                   jax.ShapeDtypeStruct((b, N_EXPERTS, n), F32)),
        grid=(b, n // tm),
        in_specs=[pl.BlockSpec((1, tm, convo.shape[2]), tok),
                  pl.BlockSpec((1, hf.shape[1], tm), lambda bi, i: (bi, 0, i)),
                  pl.BlockSpec((1, hb.shape[1], tm), lambda bi, i: (bi, 0, i)),
                  pl.BlockSpec((1, tm, d), tok),
                  pl.BlockSpec((1, N_MOD, d), mod_map),
                  pl.BlockSpec(cw.shape, cst),
                  pl.BlockSpec(cb.shape, cst),
                  pl.BlockSpec(gh.shape, cst),
                  pl.BlockSpec(wo.shape, cst),
                  pl.BlockSpec(gn2.shape, cst),
                  pl.BlockSpec(wr.shape, cst)],
        out_specs=(pl.BlockSpec((1, tm, d), tok),
                   pl.BlockSpec((1, tm, d), tok),
                   pl.BlockSpec((1, N_EXPERTS, tm), lambda bi, i: (bi, 0, i))),
        compiler_params=_params(("arbitrary", "arbitrary")),
        name="mixout",
    )(convo, hf, hb, x, mod_l, cw, cb, gh, wo, gn2, wr)


def _route_kernel(lt_ref, pos_ref, g_ref, cs_ref, aff_sc, sel_sc, *, n, cap):
    lg = lt_ref[0]
    e = jnp.exp(lg - jnp.max(lg, axis=0, keepdims=True))
    aff_sc[...] = e / jnp.sum(e, axis=0, keepdims=True)
    capf = jnp.float32(cap)

    def count_ge(cand):
        acc = jnp.zeros((N_EXPERTS, LANES), F32)
        for j in range(n // LANES):
            acc = acc + jnp.where(aff_sc[:, j * LANES:(j + 1) * LANES] >= cand, 1.0, 0.0)
        return jnp.sum(acc, axis=1, keepdims=True)

    def exp_step(_, c):
        elo, ehi = c
        emid = jnp.floor((elo + ehi) * 0.5)
        ok = count_ge(jnp.exp2(emid)) >= capf
        return jnp.where(ok, emid, elo), jnp.where(ok, ehi, emid)

    elo, ehi = lax.fori_loop(0, 8, exp_step, (jnp.full((N_EXPERTS, 1), -128.0, F32), jnp.ones((N_EXPERTS, 1), F32)))

    def val_step(_, c):
        lo, hi = c
        mid = lo + (hi - lo) * 0.5
        ok = count_ge(mid) >= capf
        return jnp.where(ok, mid, lo), jnp.where(ok, hi, mid)

    lo, hi = lax.fori_loop(0, 40, val_step, (jnp.where(elo <= -128.0, 0.0, jnp.exp2(elo)), jnp.exp2(ehi)))
    need = capf - count_ge(hi)

    row = lax.broadcasted_iota(jnp.int32, (LANES, LANES), 0)
    col = lax.broadcasted_iota(jnp.int32, (LANES, LANES), 1)
    upper = jnp.where(row < col, 1.0, 0.0).astype(BF16)
    lane = lax.broadcasted_iota(jnp.int32, (N_EXPERTS, LANES), 1)
    per_tile = ROUTE_TILE // LANES

    ceq = jnp.zeros((N_EXPERTS, 1), F32)
    for c in range(n // LANES):
        a = aff_sc[:, c * LANES:(c + 1) * LANES]
        tie = (a >= lo) & (a < hi)
        eq = jnp.where(tie, 1.0, 0.0)
        peq = jnp.dot(eq.astype(BF16), upper, preferred_element_type=F32) + ceq
        sel = (a >= hi) | (tie & (peq < need))
        g_ref[0, :, c * LANES:(c + 1) * LANES] = jnp.where(sel, a, 0.0)
        sel_sc[:, c * LANES:(c + 1) * LANES] = jnp.where(sel, 1.0, 0.0)
        ceq = ceq + jnp.sum(eq, axis=1, keepdims=True)

    csel = jnp.zeros((N_EXPERTS, 1), F32)
    offs = jnp.zeros((N_EXPERTS, LANES), jnp.int32)
    for c in range(n // LANES):
        self = sel_sc[:, c * LANES:(c + 1) * LANES]
        psel = jnp.dot(self.astype(BF16), upper, preferred_element_type=F32) + csel
        pos_ref[0, :, c * LANES:(c + 1) * LANES] = jnp.where(self > 0.0, psel.astype(jnp.int32), -1)
        if c % per_tile == 0:
            offs = jnp.where(lane == c // per_tile, csel.astype(jnp.int32), offs)
        csel = csel + jnp.sum(self, axis=1, keepdims=True)
    cs_ref[0] = jnp.where(lane >= n // ROUTE_TILE, csel.astype(jnp.int32), offs)


def _route(lt):
    b, ne, n = lt.shape
    cap = CAPACITY_FACTOR * n // N_EXPERTS
    blk = pl.BlockSpec((1, ne, n), lambda bi: (bi, 0, 0))
    return pl.pallas_call(
        functools.partial(_route_kernel, n=n, cap=cap),
        out_shape=(jax.ShapeDtypeStruct((b, ne, n), jnp.int32),
                   jax.ShapeDtypeStruct((b, ne, n), F32),
                   jax.ShapeDtypeStruct((b, ne, LANES), jnp.int32)),
        grid=(b,),
        in_specs=[blk],
        out_specs=(blk, blk, pl.BlockSpec((1, ne, LANES), lambda bi: (bi, 0, 0))),
        scratch_shapes=[pltpu.VMEM((ne, n), F32), pltpu.VMEM((ne, n), F32)],
        compiler_params=_params(("arbitrary",)),
        name="route",
    )(lt)


def _ffn_kernel(cs_ref, h2_ref, pos_ref, wg_ref, wu_ref, wd_ref, ye_ref, xe_sc, *, tdma, nt_pad, cap, row_chunk,
                group, early_rows):
    e = pl.program_id(0)
    b = pl.program_id(1)
    s = pl.program_id(2)
    ns = pl.num_programs(2)
    sub = tdma // ROUTE_TILE
    g = b % group

    @pl.when(s == 0)
    def _():
        xe_sc[g] = jnp.zeros(xe_sc.shape[1:], F32)

    cs_base = (b * N_EXPERTS + e) * nt_pad + s * sub

    jrow1 = lax.broadcasted_iota(jnp.int32, (FIRST_CHUNK, ROUTE_TILE), 0)
    starts = [cs_ref[cs_base + i] for i in range(sub + 1)]
    for i in range(sub):
        base = pl.multiple_of(_floor_to(starts[i], 8), 8)
        prow = pos_ref[0, 0, :, i * ROUTE_TILE:(i + 1) * ROUTE_TILE]
        xt = h2_ref[0, i * ROUTE_TILE:(i + 1) * ROUTE_TILE, :]
        onehot = jnp.where(prow == base + jrow1, 1.0, 0.0).astype(BF16)
        xe_sc[g, pl.ds(base, FIRST_CHUNK), :] += jnp.dot(onehot, xt, preferred_element_type=F32)

    jrow = lax.broadcasted_iota(jnp.int32, (SLOT_CHUNK, ROUTE_TILE), 0)

    def tile_body(i, carry):
        first_end = _floor_to(cs_ref[cs_base + i], 8) + FIRST_CHUNK
        c1 = cs_ref[cs_base + i + 1]
        nch = jnp.where(c1 > first_end, _ceil_div(c1 - first_end, SLOT_CHUNK), 0)
        off = pl.multiple_of(i * ROUTE_TILE, ROUTE_TILE)

        def chunk(r, c):
            rb = pl.multiple_of(first_end + r * SLOT_CHUNK, 8)
            prow = pos_ref[0, 0, :, pl.ds(off, ROUTE_TILE)]
            onehot = jnp.where(prow == rb + jrow, 1.0, 0.0).astype(BF16)
            xe_sc[g, pl.ds(rb, SLOT_CHUNK), :] += jnp.dot(onehot, h2_ref[0, pl.ds(off, ROUTE_TILE), :],
                                                          preferred_element_type=F32)
            return c

        return lax.fori_loop(0, nch, chunk, carry)

    overflow = functools.reduce(jnp.maximum, [starts[i + 1] - _floor_to(starts[i], 8) for i in range(sub)])

    @pl.when(overflow > FIRST_CHUNK)
    def _():
        lax.fori_loop(0, sub, tile_body, 0)

    def expert(xb):
        a = jnp.dot(xb, wg_ref[0], preferred_element_type=F32)
        u = jnp.dot(xb, wu_ref[0], preferred_element_type=F32)
        hm = (a * _sigmoid(a) * u).astype(BF16)
        return jnp.dot(hm, wd_ref[0], preferred_element_type=F32).astype(BF16)

    def run_rows(lo, hi):
        ye_ref[0, 0, lo:hi, :] = expert(xe_sc[0, lo:hi, :].astype(BF16))

    filled_first = cs_ref[(b * N_EXPERTS + e) * nt_pad + sub]

    if early_rows:
        @pl.when((s == 0) & (filled_first >= early_rows))
        def _():
            run_rows(0, early_rows)

    @pl.when((s == ns - 1) & (g == group - 1))
    def _():
        if group == 1:
            if early_rows:
                @pl.when(filled_first < early_rows)
                def _():
                    run_rows(0, early_rows)
            for rc in range((cap - early_rows) // row_chunk):
                run_rows(early_rows + rc * row_chunk, early_rows + (rc + 1) * row_chunk)
        else:
            y = expert(jnp.concatenate([xe_sc[i, 0:cap, :] for i in range(group)], axis=0).astype(BF16))
            for i in range(group):
                ye_ref[i, 0, 0:cap, :] = y[i * cap:(i + 1) * cap]
        for i in range(group):
            ye_ref[i, 0, cap:, :] = jnp.zeros((ye_ref.shape[2] - cap, ye_ref.shape[3]), BF16)


def _ffn(cs_flat, nt_pad, h2, pos4, wg, wu, wd):
    b, n, d = h2.shape
    ne, _, f = wg.shape
    cap = CAPACITY_FACTOR * n // N_EXPERTS
    rows = cap + SLOT_PAD
    tdma = min(n, 4096)
    row_chunk = min(cap, FFN_ROWS)
    group = b if b * cap <= FFN_ROWS else 1
    early_rows = 0
    if group == 1 and n // tdma == 2 and cap > 2 * FFN_EARLY_ROWS and (cap - FFN_EARLY_ROWS) % 32 == 0:
        early_rows = FFN_EARLY_ROWS
        row_chunk = (cap - early_rows) // 2
    grid_spec = pltpu.PrefetchScalarGridSpec(
        num_scalar_prefetch=1,
        grid=(ne, b, n // tdma),
        in_specs=[pl.BlockSpec((1, tdma, d), lambda e, bi, s, cs: (bi, s, 0)),
                  pl.BlockSpec((1, 1, 1, tdma), lambda e, bi, s, cs: (bi, e, 0, s)),
                  pl.BlockSpec((1, d, f), lambda e, bi, s, cs: (e, 0, 0)),
                  pl.BlockSpec((1, d, f), lambda e, bi, s, cs: (e, 0, 0)),
                  pl.BlockSpec((1, f, d), lambda e, bi, s, cs: (e, 0, 0))],
        out_specs=pl.BlockSpec((group, 1, rows, d), lambda e, bi, s, cs: (bi // group, e, 0, 0)),
        scratch_shapes=[pltpu.VMEM((group, rows, d), F32)])
    return pl.pallas_call(
        functools.partial(_ffn_kernel, tdma=tdma, nt_pad=nt_pad, cap=cap, row_chunk=row_chunk, group=group,
                          early_rows=early_rows),
        out_shape=jax.ShapeDtypeStruct((b, ne, rows, d), BF16),
        grid_spec=grid_spec,
        compiler_params=_params(("arbitrary", "arbitrary", "arbitrary")),
        name="ffn",
    )(cs_flat, h2, pos4, wg, wu, wd)


def _combine_kernel(cs_ref, pos_ref, g_ref, x_ref, mod_ref, gfin_ref, ye_hbm, out_ref,
                    ystage, pt_sc, acc_sc, sem, *, nt_pad, final):
    b = pl.program_id(0)
    i = pl.program_id(1)
    nt = pl.num_programs(1)
    step = b * nt + i
    slot = step % 2
    group = 4 * SLOT_CHUNK

    @pl.when(step == 0)
    def _():
        ystage[...] = jnp.zeros_like(ystage)

    def chunk_plan(bb, ii, e):
        c0 = cs_ref[(bb * N_EXPERTS + e) * nt_pad + ii]
        c1 = cs_ref[(bb * N_EXPERTS + e) * nt_pad + ii + 1]
        base = _floor_to(c0, 16)
        return base, jnp.where(c1 > c0, _ceil_div(c1 - base, SLOT_CHUNK), 0)

    def chunk_copy(bb, e, rb, sl, k):
        return pltpu.make_async_copy(ye_hbm.at[bb, e, pl.ds(rb, SLOT_CHUNK), :],
                                     ystage.at[sl, pl.ds(k * SLOT_CHUNK, SLOT_CHUNK), :], sem.at[sl])

    def extra_chunks(plans):
        return sum(jnp.maximum(nch - 1, 0) for _, nch in plans)

    def issue(bb, ii, sl):
        plans = [chunk_plan(bb, ii, e) for e in range(N_EXPERTS)]
        for e, (base, _) in enumerate(plans):
            chunk_copy(bb, e, pl.multiple_of(base, 16), sl, e).start()

        @pl.when(extra_chunks(plans) > 0)
        def _():
            k = jnp.int32(N_EXPERTS)
            for e, (base, nch) in enumerate(plans):
                def start(r, kk, e=e, base=base):
                    chunk_copy(bb, e, pl.multiple_of(base + r * SLOT_CHUNK, 16), sl, kk).start()
                    return kk + 1

                k = lax.fori_loop(1, nch, start, k)

    @pl.when(step == 0)
    def _():
        issue(b, i, slot)

    @pl.when(step + 1 < pl.num_programs(0) * nt)
    def _():
        wrap = i + 1 == nt
        issue(jnp.where(wrap, b + 1, b), jnp.where(wrap, 0, i + 1), 1 - slot)

    jrow = lax.broadcasted_iota(jnp.int32, (SLOT_CHUNK, ROUTE_TILE), 0)
    plans = [chunk_plan(b, i, e) for e in range(N_EXPERTS)]
    for e, (base, _) in enumerate(plans):
        pt_sc[e * SLOT_CHUNK:(e + 1) * SLOT_CHUNK, :] = jnp.where(
            pos_ref[0, e:e + 1, :] == base + jrow, g_ref[0, e:e + 1, :], 0.0)
    k = N_EXPERTS + extra_chunks(plans)

    @pl.when(k > N_EXPERTS)
    def _():
        kk0 = jnp.int32(N_EXPERTS)
        for e, (base, nch) in enumerate(plans):
            def weights(r, kk, e=e, base=base):
                row0 = pl.multiple_of(kk * SLOT_CHUNK, SLOT_CHUNK)
                pt_sc[pl.ds(row0, SLOT_CHUNK), :] = jnp.where(
                    pos_ref[0, e:e + 1, :] == base + r * SLOT_CHUNK + jrow, g_ref[0, e:e + 1, :], 0.0)
                return kk + 1

            kk0 = lax.fori_loop(1, nch, weights, kk0)

    kpad = jnp.maximum(_ceil_div(k, 4) * 4, COMBINE_STATIC_CHUNKS)

    def zero_pad(kk, c):
        row0 = pl.multiple_of(kk * SLOT_CHUNK, SLOT_CHUNK)
        pt_sc[pl.ds(row0, SLOT_CHUNK), :] = jnp.zeros((SLOT_CHUNK, ROUTE_TILE), F32)
        return c

    lax.fori_loop(k, kpad, zero_pad, 0)

    def wait_one(kk, c):
        chunk_copy(0, 0, 0, slot, kk).wait()
        return c

    lax.fori_loop(0, k, wait_one, 0)

    static_rows = COMBINE_STATIC_CHUNKS * SLOT_CHUNK
    w = pt_sc[0:static_rows, :].T.astype(BF16)
    acc_sc[...] = jnp.dot(w, ystage[slot, 0:static_rows, :], preferred_element_type=F32)

    def matmul_group(gi, c):
        row0 = pl.multiple_of(gi * group, group)
        wg = pt_sc[pl.ds(row0, group), :].T.astype(BF16)
        acc_sc[...] += jnp.dot(wg, ystage[slot, pl.ds(row0, group), :], preferred_element_type=F32)
        return c

    lax.fori_loop(COMBINE_STATIC_CHUNKS // 4, kpad // 4, matmul_group, 0)

    xn = x_ref[0] + mod_ref[0][5:6] * acc_sc[...]
    if final:
        ms = jnp.mean(xn * xn, axis=-1, keepdims=True)
        xn = xn * lax.rsqrt(ms + EPS) * gfin_ref[...]
    out_ref[0] = xn


def _combine(cs_flat, nt_pad, pos, g, x, mod_l, mod_row, gfin, ye, final):
    b, n, d = x.shape
    per_sample = mod_row is None
    mod_map = ((lambda bi, i, cs: (bi, 0, 0)) if per_sample else (lambda bi, i, cs: (mod_row, 0, 0)))
    max_chunks = N_EXPERTS * ((ROUTE_TILE + 15 + SLOT_CHUNK - 1) // SLOT_CHUNK) + 4
    grid_spec = pltpu.PrefetchScalarGridSpec(
        num_scalar_prefetch=1,
        grid=(b, n // ROUTE_TILE),
        in_specs=[pl.BlockSpec((1, N_EXPERTS, ROUTE_TILE), lambda bi, i, cs: (bi, 0, i)),
                  pl.BlockSpec((1, N_EXPERTS, ROUTE_TILE), lambda bi, i, cs: (bi, 0, i)),
                  pl.BlockSpec((1, ROUTE_TILE, d), lambda bi, i, cs: (bi, i, 0)),
                  pl.BlockSpec((1, N_MOD, d), mod_map),
                  pl.BlockSpec((1, d), lambda bi, i, cs: (0, 0)),
                  pl.BlockSpec(memory_space=pl.ANY)],
        out_specs=pl.BlockSpec((1, ROUTE_TILE, d), lambda bi, i, cs: (bi, i, 0)),
        scratch_shapes=[pltpu.VMEM((2, max_chunks * SLOT_CHUNK, d), BF16),
                        pltpu.VMEM((max_chunks * SLOT_CHUNK, ROUTE_TILE), F32),
                        pltpu.VMEM((ROUTE_TILE, d), F32),
                        pltpu.SemaphoreType.DMA((2,))])
    return pl.pallas_call(
        functools.partial(_combine_kernel, nt_pad=nt_pad, final=final),
        out_shape=jax.ShapeDtypeStruct((b, n, d), F32),
        grid_spec=grid_spec,
        compiler_params=_params(("arbitrary", "arbitrary")),
        name="combine",
    )(cs_flat, pos, g, x, mod_l, gfin, ye)


def _moe(lt, h2, x, mod_l, mod_row, gfin, wg, wu, wd, final):
    b, n, _ = x.shape
    pos, g, cs = _route(lt)
    nt_pad = ((n // ROUTE_TILE + 1 + 7) // 8) * 8
    cs_flat = cs[:, :, :nt_pad].reshape(-1)
    ye = _ffn(cs_flat, nt_pad, h2, pos.reshape(b, N_EXPERTS, 1, n), wg, wu, wd)
    return _combine(cs_flat, nt_pad, pos, g, x, mod_l, mod_row, gfin, ye, final)


def kernel(x, c, ctx, c_ctx, w_mod, b_mod, g_norm1, g_norm2, w_in, w_out, conv_w, conv_b, gate_b, g_head,
           w_router, w_gate_e, w_up_e, w_down_e, g_final):
    bsz, n, d = x.shape
    nctx = ctx.shape[1]
    depth = w_mod.shape[0]
    dm = N_HEADS * HEAD_DIM
    dc = conv_w.shape[2]
    ng = 4 * N_HEADS
    ctx_row = bsz

    rows = ((bsz + 1 + 7) // 8) * 8
    cc = jnp.zeros((rows, d), F32).at[:bsz].set(c).at[bsz].set(c_ctx)
    mod = _modulation(cc, w_mod, b_mod).reshape(depth, rows, N_MOD, d)
    gfin = g_final.reshape(1, d)

    xl, xc = x, ctx
    for layer in range(depth):
        last = layer == depth - 1
        mod_l = mod[layer]
        wi = w_in[layer]
        wa = jnp.concatenate([wi[:, :3 * dc], wi[:, 3 * dc + 3 * dm:3 * dc + 4 * dm]], axis=1).astype(BF16)
        wb = wi[:, 3 * dc:3 * dc + 3 * dm].astype(BF16)
        wgt = jnp.zeros((d, LANES), BF16).at[:, :ng].set(wi[:, 3 * dc + 4 * dm:].astype(BF16))
        gb = jnp.zeros((1, LANES), F32).at[0, :ng].set(gate_b[layer].reshape(-1))
        gn1 = g_norm1[layer].reshape(1, d)
        gn2 = g_norm2[layer].reshape(1, d)
        cw = jnp.zeros((8, dc), F32).at[:conv_w.shape[1]].set(conv_w[layer])
        cb = conv_b[layer].reshape(1, dc)
        gh = g_head[layer].reshape(1, dm)
        wo = w_out[layer].astype(BF16)
        wr32 = jnp.zeros((d, LANES), F32).at[:, :N_EXPERTS].set(w_router[layer])
        wr_hi = wr32.astype(BF16)
        wr = jnp.concatenate([wr_hi, (wr32 - wr_hi.astype(F32)).astype(BF16), wr_hi], axis=0)
        wge = w_gate_e[layer].astype(BF16)
        wue = w_up_e[layer].astype(BF16)
        wde = w_down_e[layer].astype(BF16)

        convo_c, qk_c, vt_c, p_c, pt_c = _inproj(xc, mod_l, ctx_row, gn1, wa, wb, wgt, gb, tm=nctx)
        convo_l, qk_l, vt_l, p_l, pt_l = _inproj(xl, mod_l, None, gn1, wa, wb, wgt, gb, tm=min(n, INPROJ_TILE))
        zero_state = _mlstm_zero_state(bsz)
        hf_c, hb_c, state = _mlstm(qk_c, vt_c, p_c, pt_c, zero_state)
        hf_l, hb_l, _ = _mlstm(qk_l, vt_l, p_l, pt_l, state)

        xl, h2_l, lt_l = _mixout(convo_l, hf_l, hb_l, xl, mod_l, None, cw, cb, gh, wo, gn2, wr,
                                 tm=min(n, MIXOUT_TILE), row_len=GRID_W)
        xl = _moe(lt_l, h2_l, xl, mod_l, None, gfin, wge, wue, wde, final=last)
        if not last:
            xc, h2_c, lt_c = _mixout(convo_c, hf_c, hb_c, xc, mod_l, ctx_row, cw, cb, gh, wo, gn2, wr,
                                     tm=nctx, row_len=nctx)
            xc = _moe(lt_c, h2_c, xc, mod_l, ctx_row, gfin, wge, wue, wde, final=False)
    return xl
```

```python
import functools

import jax
import jax.numpy as jnp
from jax import lax
from jax.experimental import pallas as pl
from jax.experimental.pallas import tpu as pltpu

EPS = 1e-6
N_HEADS = 4
HEAD_DIM = 128
CHUNK = 128
MLSTM_CHUNKS_PER_STEP = 8
INPROJ_TILE = 512
MIXOUT_TILE = 1024
N_EXPERTS = 16
CAPACITY_FACTOR = 2
N_MOD = 6
GRID_W = 64
LANES = 128
ROUTE_TILE = 256
SLOT_CHUNK = 64
FIRST_CHUNK = 64
FFN_ROWS = 512
FFN_EARLY_ROWS = 384
SLOT_PAD = 128
COMBINE_STATIC_CHUNKS = 20
VMEM_LIMIT = 56 * 1024 * 1024

F32 = jnp.float32
BF16 = jnp.bfloat16
HIGHEST = lax.Precision.HIGHEST


def _params(sem, vmem=VMEM_LIMIT):
    return pltpu.CompilerParams(dimension_semantics=sem, vmem_limit_bytes=vmem)


def _sigmoid(x):
    return 1.0 / (1.0 + jnp.exp(-x))


def _floor_to(x, m):
    return jnp.bitwise_and(x, -m)


def _ceil_div(x, m):
    return jnp.right_shift(x + (m - 1), m.bit_length() - 1)


def _mod_kernel(c_ref, w_ref, b_ref, o_ref):
    c = c_ref[...]
    s = c * _sigmoid(c)
    o_ref[0] = jnp.dot(s, w_ref[0], precision=HIGHEST, preferred_element_type=F32) + b_ref[0]


def _modulation(cc, w_mod, b_mod):
    depth, d, nm = w_mod.shape
    rows = cc.shape[0]
    tn = 1536
    return pl.pallas_call(
        _mod_kernel,
        out_shape=jax.ShapeDtypeStruct((depth, rows, nm), F32),
        grid=(depth, nm // tn),
        in_specs=[pl.BlockSpec((rows, d), lambda l, j: (0, 0)),
                  pl.BlockSpec((1, d, tn), lambda l, j: (l, 0, j)),
                  pl.BlockSpec((1, 1, tn), lambda l, j: (l, 0, j))],
        out_specs=pl.BlockSpec((1, rows, tn), lambda l, j: (l, 0, j)),
        compiler_params=_params(("arbitrary", "arbitrary")),
        name="modulation",
    )(cc, w_mod, b_mod.reshape(depth, 1, nm))


def _inproj_kernel(x_ref, mod_ref, gn_ref, wa_ref, wb_ref, wg_ref, gb_ref,
                   convo_ref, qk_ref, vt_ref, p_ref, pt_ref, *, tm):
    x = x_ref[0]
    m = mod_ref[0]
    ms = jnp.mean(x * x, axis=-1, keepdims=True)
    h = x * lax.rsqrt(ms + EPS) * (gn_ref[...] * (1.0 + m[1:2])) + m[0:1]
    hb = h.astype(BF16)
    cw = 512
    for c in range(wa_ref.shape[1] // cw):
        convo_ref[0, :, c * cw:(c + 1) * cw] = jnp.dot(
            hb, wa_ref[:, c * cw:(c + 1) * cw], preferred_element_type=F32).astype(BF16)
    q = jnp.dot(hb, wb_ref[:, 0:cw], preferred_element_type=F32) * (HEAD_DIM ** -0.5)
    qk_ref[0, :, 0:cw] = q.astype(BF16)
    qk_ref[0, :, cw:2 * cw] = jnp.dot(hb, wb_ref[:, cw:2 * cw], preferred_element_type=F32).astype(BF16)
    v = jnp.dot(hb, wb_ref[:, 2 * cw:3 * cw], preferred_element_type=F32)
    vt_ref[0] = v.T.astype(BF16)

    g = jnp.dot(hb, wg_ref[...], preferred_element_type=F32) + gb_ref[...]
    kind = lax.broadcasted_iota(jnp.int32, (1, LANES), 1) // N_HEADS
    logf = jnp.minimum(g, 0.0) - jnp.log1p(jnp.exp(-jnp.abs(g)))
    g = jnp.where((kind == 1) | (kind == 3), logf, g)
    row = lax.broadcasted_iota(jnp.int32, (CHUNK, CHUNK), 0)
    col = lax.broadcasted_iota(jnp.int32, (CHUNK, CHUNK), 1)
    tri = jnp.concatenate([jnp.where(col <= row, 1.0, 0.0), jnp.where(col >= row, 1.0, 0.0)], axis=0).astype(BF16)
    for r in range(tm // CHUNK):
        gc = g[r * CHUNK:(r + 1) * CHUNK]
        g_hi = gc.astype(BF16)
        rem = gc - g_hi.astype(F32)
        g_mid = rem.astype(BF16)
        g_lo = (rem - g_mid.astype(F32)).astype(BF16)
        sums = jnp.dot(tri, jnp.concatenate([g_hi, g_mid, g_lo], axis=1), preferred_element_type=F32)
        sums = sums[:, 0:LANES] + sums[:, LANES:2 * LANES] + sums[:, 2 * LANES:3 * LANES]
        cum, suf = sums[0:CHUNK], sums[CHUNK:2 * CHUNK]
        p = jnp.where(kind == 1, cum, jnp.where(kind == 3, suf, gc))
        pt_ref[0, :, r * CHUNK:(r + 1) * CHUNK] = p.T[:4 * N_HEADS, :]
        diff = p - pltpu.roll(p, LANES - N_HEADS, axis=1)
        p_ref[0, r * CHUNK:(r + 1) * CHUNK, :] = diff[:, :4 * N_HEADS]


def _inproj(x, mod_l, mod_row, gn, wa, wb, wg, gb, tm):
    b, n, d = x.shape
    per_sample = mod_row is None
    mod_map = (lambda bi, i: (bi, 0, 0)) if per_sample else (lambda bi, i: (mod_row, 0, 0))
    ng = 4 * N_HEADS
    dm = N_HEADS * HEAD_DIM
    return pl.pallas_call(
        functools.partial(_inproj_kernel, tm=tm),
        out_shape=(jax.ShapeDtypeStruct((b, n, wa.shape[1]), BF16),
                   jax.ShapeDtypeStruct((b, n, 2 * dm), BF16),
                   jax.ShapeDtypeStruct((b, dm, n), BF16),
                   jax.ShapeDtypeStruct((b, n, ng), F32),
                   jax.ShapeDtypeStruct((b, ng, n), F32)),
        grid=(b, n // tm),
        in_specs=[pl.BlockSpec((1, tm, d), lambda bi, i: (bi, i, 0)),
                  pl.BlockSpec((1, N_MOD, d), mod_map),
                  pl.BlockSpec((1, d), lambda bi, i: (0, 0)),
                  pl.BlockSpec(wa.shape, lambda bi, i: (0, 0)),
                  pl.BlockSpec(wb.shape, lambda bi, i: (0, 0)),
                  pl.BlockSpec(wg.shape, lambda bi, i: (0, 0)),
                  pl.BlockSpec((1, LANES), lambda bi, i: (0, 0))],
        out_specs=(pl.BlockSpec((1, tm, wa.shape[1]), lambda bi, i: (bi, i, 0)),
                   pl.BlockSpec((1, tm, 2 * dm), lambda bi, i: (bi, i, 0)),
                   pl.BlockSpec((1, dm, tm), lambda bi, i: (bi, 0, i)),
                   pl.BlockSpec((1, tm, ng), lambda bi, i: (bi, i, 0)),
                   pl.BlockSpec((1, ng, tm), lambda bi, i: (bi, 0, i))),
        compiler_params=_params(("arbitrary", "arbitrary")),
        name="inproj",
    )(x, mod_l, gn, wa, wb, wg, gb)


def _mlstm_kernel(qf_ref, qb_ref, vf_ref, vb_ref, pf_ref, pb_ref, ptf_ref, ptb_ref, c0_ref, n0_ref, m0_ref,
                  hf_ref, hb_ref, c1_ref, n1_ref, m1_ref, c_sc, n_sc, m_sc, *, chunks):
    j = pl.program_id(1)
    nc = pl.num_programs(1)
    L = CHUNK
    dm = N_HEADS * HEAD_DIM
    pad = 16

    @pl.when(j == 0)
    def _():
        c_sc[...] = c0_ref[0]
        n_sc[...] = n0_ref[0]
        m_sc[...] = m0_ref[0]

    W = 2 * HEAD_DIM
    row = lax.broadcasted_iota(jnp.int32, (L, W), 0)
    col = lax.broadcasted_iota(jnp.int32, (L, W), 1) % L
    nt_dims = (((1,), (1,)), ((), ()))
    zeros = jnp.zeros((L, HEAD_DIM), BF16)
    pairs = [(d, pr) for d in range(2) for pr in range(N_HEADS // 2)]

    def blockdiag(x):
        return jnp.concatenate([jnp.concatenate([x[:, :HEAD_DIM], zeros], axis=1),
                                jnp.concatenate([zeros, x[:, HEAD_DIM:]], axis=1)], axis=0)

    def per_head(x, lane):
        return jnp.concatenate([jnp.broadcast_to(x[:, lane:lane + 1], (1, HEAD_DIM)),
                                jnp.broadcast_to(x[:, HEAD_DIM + lane:HEAD_DIM + lane + 1], (1, HEAD_DIM))], axis=1)

    def chunk_step(sub_f, sub_b):
        tok = (slice(sub_f * L, (sub_f + 1) * L), slice(sub_b * L, (sub_b + 1) * L))
        ks, vts, cts, nvs, r1 = [], [], [], [], []
        for d, pr in pairs:
            idx = d * (N_HEADS // 2) + pr
            qk_ref = (qf_ref, qb_ref)[d]
            q = qk_ref[0, tok[d], pr * W:(pr + 1) * W]
            k = qk_ref[0, tok[d], dm + pr * W:dm + (pr + 1) * W]
            ct, nvec = c_sc[idx], n_sc[idx]
            lhs = jnp.concatenate([k, ct.astype(BF16), jnp.broadcast_to(nvec, (pad, W)).astype(BF16)], axis=0)
            r1.append(lax.dot_general(lhs, blockdiag(q), nt_dims, preferred_element_type=F32))
            v_ref = (vf_ref, vb_ref)[d]
            vts.append(jnp.concatenate([v_ref[0, (2 * pr) * HEAD_DIM:(2 * pr + 1) * HEAD_DIM, tok[d]],
                                        v_ref[0, (2 * pr + 1) * HEAD_DIM:(2 * pr + 2) * HEAD_DIM, tok[d]]], axis=1))
            ks.append(k), cts.append(ct), nvs.append(nvec)

        w_intra, w_inter, floor, w_state, m_news, r3 = [], [], [], [], [], []
        for c, (d, pr) in enumerate(pairs):
            idx = d * (N_HEADS // 2) + pr
            ci = 2 * d * N_HEADS + 2 * pr
            p = (pf_ref, pb_ref)[d][0, tok[d], :]
            pt = (ptf_ref, ptb_ref)[d][0, :, tok[d]]
            g_col = jnp.concatenate([jnp.broadcast_to(p[:, ci:ci + 1], (L, HEAD_DIM)),
                                     jnp.broadcast_to(p[:, ci + 1:ci + 2], (L, HEAD_DIM))], axis=1)
            i_row = jnp.concatenate([pt[ci:ci + 1, :], pt[ci + 1:ci + 2, :]], axis=1)
            b_row = jnp.concatenate([pt[ci + N_HEADS:ci + N_HEADS + 1, :],
                                     pt[ci + N_HEADS + 1:ci + N_HEADS + 2, :]], axis=1)
            m_prev = m_sc[idx]
            mask = (row <= col) if d == 0 else (row >= col)
            last = L - 1 if d == 0 else 0
            dmat = jnp.where(mask, b_row + g_col, -jnp.inf)
            a = b_row + m_prev
            m_t = jnp.maximum(a, jnp.max(dmat, axis=0, keepdims=True))
            w_intra.append(jnp.exp(dmat - m_t))
            w_inter.append(jnp.exp(a - m_t))
            floor.append(jnp.exp(-m_t))
            m_new = per_head(m_t, last)
            b_last = per_head(b_row, last)
            w_state.append(jnp.exp(b_last + m_prev - m_new))
            m_news.append(m_new)
            w_tok = jnp.exp(b_last - b_row + i_row - m_new)
            lhs = jnp.concatenate([(vts[c].astype(F32) * w_tok).astype(BF16),
                                   jnp.broadcast_to(w_tok, (pad, W)).astype(BF16)], axis=0)
            r3.append(jnp.dot(lhs, blockdiag(ks[c]), preferred_element_type=F32))

        for c, (d, pr) in enumerate(pairs):
            s = r1[c][0:L] * w_intra[c]
            den = w_inter[c] * r1[c][2 * L:2 * L + 1] + jnp.sum(s, axis=0, keepdims=True)
            inv = 1.0 / jnp.maximum(jnp.abs(den), floor[c])
            ht = (jnp.dot(vts[c], blockdiag(s.astype(BF16)), preferred_element_type=F32)
                  + r1[c][L:2 * L] * w_inter[c]) * inv
            out_ref = (hf_ref, hb_ref)[d]
            out_ref[0, (2 * pr) * HEAD_DIM:(2 * pr + 1) * HEAD_DIM, tok[d]] = ht[:, :HEAD_DIM]
            out_ref[0, (2 * pr + 1) * HEAD_DIM:(2 * pr + 2) * HEAD_DIM, tok[d]] = ht[:, HEAD_DIM:]

        for c, (d, pr) in enumerate(pairs):
            idx = d * (N_HEADS // 2) + pr
            c_sc[idx] = w_state[c] * cts[c] + r3[c][0:HEAD_DIM]
            n_sc[idx] = w_state[c] * nvs[c] + r3[c][HEAD_DIM:HEAD_DIM + 1]
            m_sc[idx] = m_news[c]

    for u in range(chunks):
        chunk_step(u, chunks - 1 - u)

    @pl.when(j == nc - 1)
    def _():
        c1_ref[0] = c_sc[...]
        n1_ref[0] = n_sc[...]
        m1_ref[0] = m_sc[...]


def _mlstm_zero_state(bsz):
    npair = N_HEADS
    return (jnp.zeros((bsz, npair, HEAD_DIM, 2 * HEAD_DIM), F32),
            jnp.zeros((bsz, npair, 1, 2 * HEAD_DIM), F32),
            jnp.zeros((bsz, npair, 1, 2 * HEAD_DIM), F32))


def _mlstm(qk, vt, p, pt, state):
    b, n, w = qk.shape
    chunks = min(MLSTM_CHUNKS_PER_STEP, n // CHUNK)
    span = CHUNK * chunks
    nc = n // span
    ng = 4 * N_HEADS
    dm = N_HEADS * HEAD_DIM
    c0, n0, m0 = state
    fwd = lambda bi, j: (bi, j, 0)
    bwd = lambda bi, j: (bi, nc - 1 - j, 0)
    fwd_t = lambda bi, j: (bi, 0, j)
    bwd_t = lambda bi, j: (bi, 0, nc - 1 - j)
    st4 = lambda bi, j: (bi, 0, 0, 0)
    outs = pl.pallas_call(
        functools.partial(_mlstm_kernel, chunks=chunks),
        out_shape=(jax.ShapeDtypeStruct((b, dm, n), F32),
                   jax.ShapeDtypeStruct((b, dm, n), F32),
                   jax.ShapeDtypeStruct(c0.shape, F32),
                   jax.ShapeDtypeStruct(n0.shape, F32),
                   jax.ShapeDtypeStruct(m0.shape, F32)),
        grid=(b, nc),
        in_specs=[pl.BlockSpec((1, span, w), fwd),
                  pl.BlockSpec((1, span, w), bwd),
                  pl.BlockSpec((1, dm, span), fwd_t),
                  pl.BlockSpec((1, dm, span), bwd_t),
                  pl.BlockSpec((1, span, ng), fwd),
                  pl.BlockSpec((1, span, ng), bwd),
                  pl.BlockSpec((1, ng, span), fwd_t),
                  pl.BlockSpec((1, ng, span), bwd_t),
                  pl.BlockSpec((1,) + c0.shape[1:], st4),
                  pl.BlockSpec((1,) + n0.shape[1:], st4),
                  pl.BlockSpec((1,) + m0.shape[1:], st4)],
        out_specs=(pl.BlockSpec((1, dm, span), fwd_t),
                   pl.BlockSpec((1, dm, span), bwd_t),
                   pl.BlockSpec((1,) + c0.shape[1:], st4),
                   pl.BlockSpec((1,) + n0.shape[1:], st4),
                   pl.BlockSpec((1,) + m0.shape[1:], st4)),
        scratch_shapes=[pltpu.VMEM(c0.shape[1:], F32),
                        pltpu.VMEM(n0.shape[1:], F32),
                        pltpu.VMEM(m0.shape[1:], F32)],
        compiler_params=_params(("arbitrary", "arbitrary")),
        name="mlstm",
    )(qk, qk, vt, vt, p, p, pt, pt, c0, n0, m0)
    return outs[0], outs[1], (outs[2], outs[3], outs[4])


def _mixout_kernel(convo_ref, hf_ref, hb_ref, x_ref, mod_ref, cw_ref, cb_ref, gh_ref, wo_ref, gn2_ref, wr_ref,
                   xo_ref, h2_ref, lt_ref, *, tm, row_len):
    dc = cw_ref.shape[1]
    cw = cw_ref[...]
    m = mod_ref[0]
    gain2 = gn2_ref[...] * (1.0 + m[4:5])
    sub = LANES if LANES % row_len == 0 else tm
    t = lax.broadcasted_iota(jnp.int32, (sub, dc), 0) % row_len
    for r in range(tm // sub):
        rows = slice(r * sub, (r + 1) * sub)
        cv = convo_ref[0, rows, :]
        bg = cv[:, 0:dc].astype(F32)
        cg = cv[:, dc:2 * dc].astype(F32)
        xi = cv[:, 2 * dc:3 * dc].astype(F32)
        og = cv[:, 3 * dc:].astype(F32)
        u = cg * xi
        u_prev = jnp.where(t == 0, 0.0, pltpu.roll(u, 1, axis=0))
        u_next = jnp.where(t == row_len - 1, 0.0, pltpu.roll(u, sub - 1, axis=0))
        yc = bg * (cw[0:1] * u_prev + cw[1:2] * u + cw[2:3] * u_next + cb_ref[...])

        hm = hf_ref[0, :, rows] + hb_ref[0, :, rows]
        parts = []
        for hd in range(N_HEADS):
            hh = hm[hd * HEAD_DIM:(hd + 1) * HEAD_DIM, :]
            parts.append(hh * lax.rsqrt(jnp.mean(hh * hh, axis=0, keepdims=True) + EPS))
        ym = _sigmoid(og) * (jnp.concatenate(parts, axis=0).T * gh_ref[...])

        cat = jnp.concatenate([yc, ym], axis=-1).astype(BF16)
        y = jnp.dot(cat, wo_ref[...], preferred_element_type=F32)
        xn = x_ref[0, rows, :] + m[2:3] * y
        xo_ref[0, rows, :] = xn
        ms = jnp.mean(xn * xn, axis=-1, keepdims=True)
        h2 = xn * lax.rsqrt(ms + EPS) * gain2 + m[3:4]
        hi = h2.astype(BF16)
        h2_ref[0, rows, :] = hi
        lo = (h2 - hi.astype(F32)).astype(BF16)
        lg = jnp.dot(jnp.concatenate([hi, hi, lo], axis=1), wr_ref[...], preferred_element_type=F32)
        for q in range(sub // LANES):
            lanes = slice(r * sub + q * LANES, r * sub + (q + 1) * LANES)
            lt_ref[0, :, lanes] = lg[q * LANES:(q + 1) * LANES, :].T[:N_EXPERTS, :]


def _mixout(convo, hf, hb, x, mod_l, mod_row, cw, cb, gh, wo, gn2, wr, tm, row_len):
    b, n, d = x.shape
    per_sample = mod_row is None
    mod_map = (lambda bi, i: (bi, 0, 0)) if per_sample else (lambda bi, i: (mod_row, 0, 0))
    tok = lambda bi, i: (bi, i, 0)
    cst = lambda bi, i: (0, 0)
    return pl.pallas_call(
        functools.partial(_mixout_kernel, tm=tm, row_len=row_len),
        out_shape=(jax.ShapeDtypeStruct((b, n, d), F32),
                   jax.ShapeDtypeStruct((b, n, d), BF16),
                   jax.ShapeDtypeStruct((b, N_EXPERTS, n), F32)),
        grid=(b, n // tm),
        in_specs=[pl.BlockSpec((1, tm, convo.shape[2]), tok),
                  pl.BlockSpec((1, hf.shape[1], tm), lambda bi, i: (bi, 0, i)),
                  pl.BlockSpec((1, hb.shape[1], tm), lambda bi, i: (bi, 0, i)),
                  pl.BlockSpec((1, tm, d), tok),
                  pl.BlockSpec((1, N_MOD, d), mod_map),
                  pl.BlockSpec(cw.shape, cst),
                  pl.BlockSpec(cb.shape, cst),
                  pl.BlockSpec(gh.shape, cst),
                  pl.BlockSpec(wo.shape, cst),
                  pl.BlockSpec(gn2.shape, cst),
                  pl.BlockSpec(wr.shape, cst)],
        out_specs=(pl.BlockSpec((1, tm, d), tok),
                   pl.BlockSpec((1, tm, d), tok),
                   pl.BlockSpec((1, N_EXPERTS, tm), lambda bi, i: (bi, 0, i))),
        compiler_params=_params(("arbitrary", "arbitrary")),
        name="mixout",
    )(convo, hf, hb, x, mod_l, cw, cb, gh, wo, gn2, wr)


def _route_kernel(lt_ref, pos_ref, g_ref, cs_ref, aff_sc, sel_sc, *, n, cap):
    lg = lt_ref[0]
    e = jnp.exp(lg - jnp.max(lg, axis=0, keepdims=True))
    aff_sc[...] = e / jnp.sum(e, axis=0, keepdims=True)
    capf = jnp.float32(cap)

    def count_ge(cand):
        acc = jnp.zeros((N_EXPERTS, LANES), F32)
        for j in range(n // LANES):
            acc = acc + jnp.where(aff_sc[:, j * LANES:(j + 1) * LANES] >= cand, 1.0, 0.0)
        return jnp.sum(acc, axis=1, keepdims=True)

    def exp_step(_, c):
        elo, ehi = c
        emid = jnp.floor((elo + ehi) * 0.5)
        ok = count_ge(jnp.exp2(emid)) >= capf
        return jnp.where(ok, emid, elo), jnp.where(ok, ehi, emid)

    elo, ehi = lax.fori_loop(0, 8, exp_step, (jnp.full((N_EXPERTS, 1), -128.0, F32), jnp.ones((N_EXPERTS, 1), F32)))

    def val_step(_, c):
        lo, hi = c
        mid = lo + (hi - lo) * 0.5
        ok = count_ge(mid) >= capf
        return jnp.where(ok, mid, lo), jnp.where(ok, hi, mid)

    lo, hi = lax.fori_loop(0, 40, val_step, (jnp.where(elo <= -128.0, 0.0, jnp.exp2(elo)), jnp.exp2(ehi)))
    need = capf - count_ge(hi)

    row = lax.broadcasted_iota(jnp.int32, (LANES, LANES), 0)
    col = lax.broadcasted_iota(jnp.int32, (LANES, LANES), 1)
    upper = jnp.where(row < col, 1.0, 0.0).astype(BF16)
    lane = lax.broadcasted_iota(jnp.int32, (N_EXPERTS, LANES), 1)
    per_tile = ROUTE_TILE // LANES

    ceq = jnp.zeros((N_EXPERTS, 1), F32)
    for c in range(n // LANES):
        a = aff_sc[:, c * LANES:(c + 1) * LANES]
        tie = (a >= lo) & (a < hi)
        eq = jnp.where(tie, 1.0, 0.0)
        peq = jnp.dot(eq.astype(BF16), upper, preferred_element_type=F32) + ceq
        sel = (a >= hi) | (tie & (peq < need))
        g_ref[0, :, c * LANES:(c + 1) * LANES] = jnp.where(sel, a, 0.0)
        sel_sc[:, c * LANES:(c + 1) * LANES] = jnp.where(sel, 1.0, 0.0)
        ceq = ceq + jnp.sum(eq, axis=1, keepdims=True)

    csel = jnp.zeros((N_EXPERTS, 1), F32)
    offs = jnp.zeros((N_EXPERTS, LANES), jnp.int32)
    for c in range(n // LANES):
        self = sel_sc[:, c * LANES:(c + 1) * LANES]
        psel = jnp.dot(self.astype(BF16), upper, preferred_element_type=F32) + csel
        pos_ref[0, :, c * LANES:(c + 1) * LANES] = jnp.where(self > 0.0, psel.astype(jnp.int32), -1)
        if c % per_tile == 0:
            offs = jnp.where(lane == c // per_tile, csel.astype(jnp.int32), offs)
        csel = csel + jnp.sum(self, axis=1, keepdims=True)
    cs_ref[0] = jnp.where(lane >= n // ROUTE_TILE, csel.astype(jnp.int32), offs)


def _route(lt):
    b, ne, n = lt.shape
    cap = CAPACITY_FACTOR * n // N_EXPERTS
    blk = pl.BlockSpec((1, ne, n), lambda bi: (bi, 0, 0))
    return pl.pallas_call(
        functools.partial(_route_kernel, n=n, cap=cap),
        out_shape=(jax.ShapeDtypeStruct((b, ne, n), jnp.int32),
                   jax.ShapeDtypeStruct((b, ne, n), F32),
                   jax.ShapeDtypeStruct((b, ne, LANES), jnp.int32)),
        grid=(b,),
        in_specs=[blk],
        out_specs=(blk, blk, pl.BlockSpec((1, ne, LANES), lambda bi: (bi, 0, 0))),
        scratch_shapes=[pltpu.VMEM((ne, n), F32), pltpu.VMEM((ne, n), F32)],
        compiler_params=_params(("arbitrary",)),
        name="route",
    )(lt)


def _ffn_kernel(cs_ref, h2_ref, pos_ref, wg_ref, wu_ref, wd_ref, ye_ref, xe_sc, *, tdma, nt_pad, cap, row_chunk,
                group, early_rows):
    e = pl.program_id(0)
    b = pl.program_id(1)
    s = pl.program_id(2)
    ns = pl.num_programs(2)
    sub = tdma // ROUTE_TILE
    g = b % group

    @pl.when(s == 0)
    def _():
        xe_sc[g] = jnp.zeros(xe_sc.shape[1:], F32)

    cs_base = (b * N_EXPERTS + e) * nt_pad + s * sub

    jrow1 = lax.broadcasted_iota(jnp.int32, (FIRST_CHUNK, ROUTE_TILE), 0)
    starts = [cs_ref[cs_base + i] for i in range(sub + 1)]
    for i in range(sub):
        base = pl.multiple_of(_floor_to(starts[i], 8), 8)
        prow = pos_ref[0, 0, :, i * ROUTE_TILE:(i + 1) * ROUTE_TILE]
        xt = h2_ref[0, i * ROUTE_TILE:(i + 1) * ROUTE_TILE, :]
        onehot = jnp.where(prow == base + jrow1, 1.0, 0.0).astype(BF16)
        xe_sc[g, pl.ds(base, FIRST_CHUNK), :] += jnp.dot(onehot, xt, preferred_element_type=F32)

    jrow = lax.broadcasted_iota(jnp.int32, (SLOT_CHUNK, ROUTE_TILE), 0)

    def tile_body(i, carry):
        first_end = _floor_to(cs_ref[cs_base + i], 8) + FIRST_CHUNK
        c1 = cs_ref[cs_base + i + 1]
        nch = jnp.where(c1 > first_end, _ceil_div(c1 - first_end, SLOT_CHUNK), 0)
        off = pl.multiple_of(i * ROUTE_TILE, ROUTE_TILE)

        def chunk(r, c):
            rb = pl.multiple_of(first_end + r * SLOT_CHUNK, 8)
            prow = pos_ref[0, 0, :, pl.ds(off, ROUTE_TILE)]
            onehot = jnp.where(prow == rb + jrow, 1.0, 0.0).astype(BF16)
            xe_sc[g, pl.ds(rb, SLOT_CHUNK), :] += jnp.dot(onehot, h2_ref[0, pl.ds(off, ROUTE_TILE), :],
                                                          preferred_element_type=F32)
            return c

        return lax.fori_loop(0, nch, chunk, carry)

    overflow = functools.reduce(jnp.maximum, [starts[i + 1] - _floor_to(starts[i], 8) for i in range(sub)])

    @pl.when(overflow > FIRST_CHUNK)
    def _():
        lax.fori_loop(0, sub, tile_body, 0)

    def expert(xb):
        a = jnp.dot(xb, wg_ref[0], preferred_element_type=F32)
        u = jnp.dot(xb, wu_ref[0], preferred_element_type=F32)
        hm = (a * _sigmoid(a) * u).astype(BF16)
        return jnp.dot(hm, wd_ref[0], preferred_element_type=F32).astype(BF16)

    def run_rows(lo, hi):
        ye_ref[0, 0, lo:hi, :] = expert(xe_sc[0, lo:hi, :].astype(BF16))

    filled_first = cs_ref[(b * N_EXPERTS + e) * nt_pad + sub]

    if early_rows:
        @pl.when((s == 0) & (filled_first >= early_rows))
        def _():
            run_rows(0, early_rows)

    @pl.when((s == ns - 1) & (g == group - 1))
    def _():
        if group == 1:
            if early_rows:
                @pl.when(filled_first < early_rows)
                def _():
                    run_rows(0, early_rows)
            for rc in range((cap - early_rows) // row_chunk):
                run_rows(early_rows + rc * row_chunk, early_rows + (rc + 1) * row_chunk)
        else:
            y = expert(jnp.concatenate([xe_sc[i, 0:cap, :] for i in range(group)], axis=0).astype(BF16))
            for i in range(group):
                ye_ref[i, 0, 0:cap, :] = y[i * cap:(i + 1) * cap]
        for i in range(group):
            ye_ref[i, 0, cap:, :] = jnp.zeros((ye_ref.shape[2] - cap, ye_ref.shape[3]), BF16)


def _ffn(cs_flat, nt_pad, h2, pos4, wg, wu, wd):
    b, n, d = h2.shape
    ne, _, f = wg.shape
    cap = CAPACITY_FACTOR * n // N_EXPERTS
    rows = cap + SLOT_PAD
    tdma = min(n, 8192)
    row_chunk = min(cap, FFN_ROWS)
    group = b if b * cap <= FFN_ROWS else 1
    early_rows = 0
    if group == 1 and n // tdma == 2 and cap > 2 * FFN_EARLY_ROWS and (cap - FFN_EARLY_ROWS) % 32 == 0:
        early_rows = FFN_EARLY_ROWS
        row_chunk = (cap - early_rows) // 2
    grid_spec = pltpu.PrefetchScalarGridSpec(
        num_scalar_prefetch=1,
        grid=(ne, b, n // tdma),
        in_specs=[pl.BlockSpec((1, tdma, d), lambda e, bi, s, cs: (bi, s, 0)),
                  pl.BlockSpec((1, 1, 1, tdma), lambda e, bi, s, cs: (bi, e, 0, s)),
                  pl.BlockSpec((1, d, f), lambda e, bi, s, cs: (e, 0, 0), pipeline_mode=pl.Buffered(1)),
                  pl.BlockSpec((1, d, f), lambda e, bi, s, cs: (e, 0, 0), pipeline_mode=pl.Buffered(1)),
                  pl.BlockSpec((1, f, d), lambda e, bi, s, cs: (e, 0, 0), pipeline_mode=pl.Buffered(1))],
        out_specs=pl.BlockSpec((group, 1, rows, d), lambda e, bi, s, cs: (bi // group, e, 0, 0)),
        scratch_shapes=[pltpu.VMEM((group, rows, d), F32)])
    return pl.pallas_call(
        functools.partial(_ffn_kernel, tdma=tdma, nt_pad=nt_pad, cap=cap, row_chunk=row_chunk, group=group,
                          early_rows=early_rows),
        out_shape=jax.ShapeDtypeStruct((b, ne, rows, d), BF16),
        grid_spec=grid_spec,
        compiler_params=_params(("arbitrary", "arbitrary", "arbitrary")),
        name="ffn",
    )(cs_flat, h2, pos4, wg, wu, wd)


def _combine_kernel(cs_ref, pos_ref, g_ref, x_ref, mod_ref, gfin_ref, ye_hbm, out_ref,
                    ystage, pt_sc, acc_sc, sem, *, nt_pad, final):
    b = pl.program_id(0)
    i = pl.program_id(1)
    nt = pl.num_programs(1)
    step = b * nt + i
    slot = step % 2
    group = 4 * SLOT_CHUNK

    @pl.when(step == 0)
    def _():
        ystage[...] = jnp.zeros_like(ystage)

    def chunk_plan(bb, ii, e):
        c0 = cs_ref[(bb * N_EXPERTS + e) * nt_pad + ii]
        c1 = cs_ref[(bb * N_EXPERTS + e) * nt_pad + ii + 1]
        base = _floor_to(c0, 16)
        return base, jnp.where(c1 > c0, _ceil_div(c1 - base, SLOT_CHUNK), 0)

    def chunk_copy(bb, e, rb, sl, k):
        return pltpu.make_async_copy(ye_hbm.at[bb, e, pl.ds(rb, SLOT_CHUNK), :],
                                     ystage.at[sl, pl.ds(k * SLOT_CHUNK, SLOT_CHUNK), :], sem.at[sl])

    def extra_chunks(plans):
        return sum(jnp.maximum(nch - 1, 0) for _, nch in plans)

    def issue(bb, ii, sl):
        plans = [chunk_plan(bb, ii, e) for e in range(N_EXPERTS)]
        for e, (base, _) in enumerate(plans):
            chunk_copy(bb, e, pl.multiple_of(base, 16), sl, e).start()

        @pl.when(extra_chunks(plans) > 0)
        def _():
            k = jnp.int32(N_EXPERTS)
            for e, (base, nch) in enumerate(plans):
                def start(r, kk, e=e, base=base):
                    chunk_copy(bb, e, pl.multiple_of(base + r * SLOT_CHUNK, 16), sl, kk).start()
                    return kk + 1

                k = lax.fori_loop(1, nch, start, k)

    @pl.when(step == 0)
    def _():
        issue(b, i, slot)

    @pl.when(step + 1 < pl.num_programs(0) * nt)
    def _():
        wrap = i + 1 == nt
        issue(jnp.where(wrap, b + 1, b), jnp.where(wrap, 0, i + 1), 1 - slot)

    jrow = lax.broadcasted_iota(jnp.int32, (SLOT_CHUNK, ROUTE_TILE), 0)
    plans = [chunk_plan(b, i, e) for e in range(N_EXPERTS)]
    for e, (base, _) in enumerate(plans):
        pt_sc[e * SLOT_CHUNK:(e + 1) * SLOT_CHUNK, :] = jnp.where(
            pos_ref[0, e:e + 1, :] == base + jrow, g_ref[0, e:e + 1, :], 0.0)
    k = N_EXPERTS + extra_chunks(plans)

    @pl.when(k > N_EXPERTS)
    def _():
        kk0 = jnp.int32(N_EXPERTS)
        for e, (base, nch) in enumerate(plans):
            def weights(r, kk, e=e, base=base):
                row0 = pl.multiple_of(kk * SLOT_CHUNK, SLOT_CHUNK)
                pt_sc[pl.ds(row0, SLOT_CHUNK), :] = jnp.where(
                    pos_ref[0, e:e + 1, :] == base + r * SLOT_CHUNK + jrow, g_ref[0, e:e + 1, :], 0.0)
                return kk + 1

            kk0 = lax.fori_loop(1, nch, weights, kk0)

    kpad = jnp.maximum(_ceil_div(k, 4) * 4, COMBINE_STATIC_CHUNKS)

    def zero_pad(kk, c):
        row0 = pl.multiple_of(kk * SLOT_CHUNK, SLOT_CHUNK)
        pt_sc[pl.ds(row0, SLOT_CHUNK), :] = jnp.zeros((SLOT_CHUNK, ROUTE_TILE), F32)
        return c

    lax.fori_loop(k, kpad, zero_pad, 0)

    def wait_one(kk, c):
        chunk_copy(0, 0, 0, slot, kk).wait()
        return c

    lax.fori_loop(0, k, wait_one, 0)

    static_rows = COMBINE_STATIC_CHUNKS * SLOT_CHUNK
    w = pt_sc[0:static_rows, :].T.astype(BF16)
    acc_sc[...] = jnp.dot(w, ystage[slot, 0:static_rows, :], preferred_element_type=F32)

    def matmul_group(gi, c):
        row0 = pl.multiple_of(gi * group, group)
        wg = pt_sc[pl.ds(row0, group), :].T.astype(BF16)
        acc_sc[...] += jnp.dot(wg, ystage[slot, pl.ds(row0, group), :], preferred_element_type=F32)
        return c

    lax.fori_loop(COMBINE_STATIC_CHUNKS // 4, kpad // 4, matmul_group, 0)

    xn = x_ref[0] + mod_ref[0][5:6] * acc_sc[...]
    if final:
        ms = jnp.mean(xn * xn, axis=-1, keepdims=True)
        xn = xn * lax.rsqrt(ms + EPS) * gfin_ref[...]
    out_ref[0] = xn


def _combine(cs_flat, nt_pad, pos, g, x, mod_l, mod_row, gfin, ye, final):
    b, n, d = x.shape
    per_sample = mod_row is None
    mod_map = ((lambda bi, i, cs: (bi, 0, 0)) if per_sample else (lambda bi, i, cs: (mod_row, 0, 0)))
    max_chunks = N_EXPERTS * ((ROUTE_TILE + 15 + SLOT_CHUNK - 1) // SLOT_CHUNK) + 4
    grid_spec = pltpu.PrefetchScalarGridSpec(
        num_scalar_prefetch=1,
        grid=(b, n // ROUTE_TILE),
        in_specs=[pl.BlockSpec((1, N_EXPERTS, ROUTE_TILE), lambda bi, i, cs: (bi, 0, i)),
                  pl.BlockSpec((1, N_EXPERTS, ROUTE_TILE), lambda bi, i, cs: (bi, 0, i)),
                  pl.BlockSpec((1, ROUTE_TILE, d), lambda bi, i, cs: (bi, i, 0)),
                  pl.BlockSpec((1, N_MOD, d), mod_map),
                  pl.BlockSpec((1, d), lambda bi, i, cs: (0, 0)),
                  pl.BlockSpec(memory_space=pl.ANY)],
        out_specs=pl.BlockSpec((1, ROUTE_TILE, d), lambda bi, i, cs: (bi, i, 0)),
        scratch_shapes=[pltpu.VMEM((2, max_chunks * SLOT_CHUNK, d), BF16),
                        pltpu.VMEM((max_chunks * SLOT_CHUNK, ROUTE_TILE), F32),
                        pltpu.VMEM((ROUTE_TILE, d), F32),
                        pltpu.SemaphoreType.DMA((2,))])
    return pl.pallas_call(
        functools.partial(_combine_kernel, nt_pad=nt_pad, final=final),
        out_shape=jax.ShapeDtypeStruct((b, n, d), F32),
        grid_spec=grid_spec,
        compiler_params=_params(("arbitrary", "arbitrary")),
        name="combine",
    )(cs_flat, pos, g, x, mod_l, gfin, ye)


def _moe(lt, h2, x, mod_l, mod_row, gfin, wg, wu, wd, final):
    b, n, _ = x.shape
    pos, g, cs = _route(lt)
    nt_pad = ((n // ROUTE_TILE + 1 + 7) // 8) * 8
    cs_flat = cs[:, :, :nt_pad].reshape(-1)
    ye = _ffn(cs_flat, nt_pad, h2, pos.reshape(b, N_EXPERTS, 1, n), wg, wu, wd)
    return _combine(cs_flat, nt_pad, pos, g, x, mod_l, mod_row, gfin, ye, final)


def kernel(x, c, ctx, c_ctx, w_mod, b_mod, g_norm1, g_norm2, w_in, w_out, conv_w, conv_b, gate_b, g_head,
           w_router, w_gate_e, w_up_e, w_down_e, g_final):
    bsz, n, d = x.shape
    nctx = ctx.shape[1]
    depth = w_mod.shape[0]
    dm = N_HEADS * HEAD_DIM
    dc = conv_w.shape[2]
    ng = 4 * N_HEADS
    ctx_row = bsz

    rows = ((bsz + 1 + 7) // 8) * 8
    cc = jnp.zeros((rows, d), F32).at[:bsz].set(c).at[bsz].set(c_ctx)
    mod = _modulation(cc, w_mod, b_mod).reshape(depth, rows, N_MOD, d)
    gfin = g_final.reshape(1, d)

    xl, xc = x, ctx
    for layer in range(depth):
        last = layer == depth - 1
        mod_l = mod[layer]
        wi = w_in[layer]
        wa = jnp.concatenate([wi[:, :3 * dc], wi[:, 3 * dc + 3 * dm:3 * dc + 4 * dm]], axis=1).astype(BF16)
        wb = wi[:, 3 * dc:3 * dc + 3 * dm].astype(BF16)
        wgt = jnp.zeros((d, LANES), BF16).at[:, :ng].set(wi[:, 3 * dc + 4 * dm:].astype(BF16))
        gb = jnp.zeros((1, LANES), F32).at[0, :ng].set(gate_b[layer].reshape(-1))
        gn1 = g_norm1[layer].reshape(1, d)
        gn2 = g_norm2[layer].reshape(1, d)
        cw = jnp.zeros((8, dc), F32).at[:conv_w.shape[1]].set(conv_w[layer])
        cb = conv_b[layer].reshape(1, dc)
        gh = g_head[layer].reshape(1, dm)
        wo = w_out[layer].astype(BF16)
        wr32 = jnp.zeros((d, LANES), F32).at[:, :N_EXPERTS].set(w_router[layer])
        wr_hi = wr32.astype(BF16)
        wr = jnp.concatenate([wr_hi, (wr32 - wr_hi.astype(F32)).astype(BF16), wr_hi], axis=0)
        wge = w_gate_e[layer].astype(BF16)
        wue = w_up_e[layer].astype(BF16)
        wde = w_down_e[layer].astype(BF16)

        convo_c, qk_c, vt_c, p_c, pt_c = _inproj(xc, mod_l, ctx_row, gn1, wa, wb, wgt, gb, tm=nctx)
        convo_l, qk_l, vt_l, p_l, pt_l = _inproj(xl, mod_l, None, gn1, wa, wb, wgt, gb, tm=min(n, INPROJ_TILE))
        zero_state = _mlstm_zero_state(bsz)
        hf_c, hb_c, state = _mlstm(qk_c, vt_c, p_c, pt_c, zero_state)
        hf_l, hb_l, _ = _mlstm(qk_l, vt_l, p_l, pt_l, state)

        xl, h2_l, lt_l = _mixout(convo_l, hf_l, hb_l, xl, mod_l, None, cw, cb, gh, wo, gn2, wr,
                                 tm=min(n, MIXOUT_TILE), row_len=GRID_W)
        xl = _moe(lt_l, h2_l, xl, mod_l, None, gfin, wge, wue, wde, final=last)
        if not last:
            xc, h2_c, lt_c = _mixout(convo_c, hf_c, hb_c, xc, mod_l, ctx_row, cw, cb, gh, wo, gn2, wr,
                                     tm=nctx, row_len=nctx)
            xc = _moe(lt_c, h2_c, xc, mod_l, ctx_row, gfin, wge, wue, wde, final=False)
    return xl
```
